```python
import math
import jax, jax.numpy as jnp
from jax import lax
import numpy as np

D_MODEL = 2048
BATCH = 1
SEQ = 16384
DEPTH = 2

N_MEM = 256
MIX_W = D_MODEL // 2
GLA_HEADS = 4
GLA_DV = MIX_W // GLA_HEADS
GLA_DK = GLA_DV // 2
GLA_RANK = 16
GLA_TAU = 16.0
GLA_CHUNK = 64
SB_HEADS = 8
SB_DH = MIX_W // SB_HEADS
SB_BLOCK = 128
MEM_HEADS = 4
MEM_DH = MIX_W // MEM_HEADS
N_BRANCH = 3
D_FF = int(math.ceil(8 * D_MODEL / 3 / 256) * 256)
EPS = 1e-6

SPLIT_SIZES = [
    GLA_HEADS * GLA_DK,
    GLA_HEADS * GLA_DK,
    GLA_HEADS * GLA_DV,
    GLA_HEADS * GLA_DV,
    GLA_RANK,
    SB_HEADS * SB_DH,
    SB_HEADS * SB_DH,
    SB_HEADS * SB_DH,
    MEM_HEADS * MEM_DH,
    N_BRANCH * D_MODEL,
]
IN_W = int(sum(SPLIT_SIZES))
SPLIT_POINTS = [int(p) for p in np.cumsum(SPLIT_SIZES)[:-1]]

kernel_name = "hybrid_gla_stickbreaking_memxattn_gated"


def rmsnorm(x, g):
    xf = x.astype(jnp.float32)
    y = xf * lax.rsqrt(jnp.mean(xf * xf, axis=-1, keepdims=True) + EPS)
    return (y * g.astype(jnp.float32)).astype(x.dtype)


def gla_mixer(q, k, v, log_a):
    B, T, H, dk = q.shape
    dv = v.shape[-1]
    N = T // GLA_CHUNK

    def chunk(a):
        return a.astype(jnp.float32).reshape(B, N, GLA_CHUNK, H, a.shape[-1]).transpose(0, 3, 1, 2, 4)

    qc = chunk(q) * (dk ** -0.5)
    kc, vc, gc = chunk(k), chunk(v), chunk(log_a)
    b = jnp.cumsum(gc, axis=3)
    b_last = b[:, :, :, -1:, :]
    q_e = qc * jnp.exp(b)
    k_e = kc * jnp.exp(-b)
    k_d = kc * jnp.exp(b_last - b)
    decay = jnp.exp(b_last[:, :, :, 0, :])

    causal = jnp.tril(jnp.ones((GLA_CHUNK, GLA_CHUNK), dtype=bool))
    s = jnp.where(causal, jnp.einsum('bhncd,bhnsd->bhncs', q_e, k_e), 0.0)
    o_intra = jnp.einsum('bhncs,bhnsv->bhncv', s, vc)

    def step(S, inp):
        qe, kd, vv, dec = inp
        o = jnp.einsum('bhcd,bhdv->bhcv', qe, S)
        S = S * dec[..., :, None] + jnp.einsum('bhcd,bhcv->bhdv', kd, vv)
        return S, o

    xs = (jnp.moveaxis(q_e, 2, 0), jnp.moveaxis(k_d, 2, 0), jnp.moveaxis(vc, 2, 0), jnp.moveaxis(decay, 2, 0))
    S0 = jnp.zeros((B, H, dk, dv), jnp.float32)
    _, o_inter = lax.scan(step, S0, xs)
    o = o_intra + jnp.moveaxis(o_inter, 0, 2)
    return o.transpose(0, 2, 3, 1, 4).reshape(B, T, H, dv)


def stick_breaking_mixer(q, k, v):
    B, T, H, d = q.shape
    NB = T // SB_BLOCK
    scale = d ** -0.5
    qt = q.transpose(0, 2, 1, 3)
    kt = k.transpose(0, 2, 1, 3)
    vt = v.transpose(0, 2, 1, 3)
    idx = jnp.arange(SB_BLOCK)
    later = (idx[:, None] > idx[None, :]).astype(jnp.float32)
    outs = []
    for i in range(NB):
        L = (i + 1) * SB_BLOCK
        qi = qt[:, :, i * SB_BLOCK:L]
        kb = kt[:, :, :L]
        vb = vt[:, :, :L]
        z = jnp.einsum('bhqd,bhkd->bhqk', qi, kb).astype(jnp.float32) * scale
        mask = jnp.arange(L)[None, :] < (i * SB_BLOCK + idx)[:, None]
        l = jnp.where(mask, jax.nn.log_sigmoid(-z), 0.0).reshape(B, H, SB_BLOCK, i + 1, SB_BLOCK)
        within = jnp.einsum('bhqnj,js->bhqns', l, later)
        tot = jnp.sum(l, axis=-1)
        after = lax.cumsum(tot, axis=3, reverse=True) - tot
        rest = (within + after[..., None]).reshape(B, H, SB_BLOCK, L)
        A = jnp.where(mask, jnp.exp(jax.nn.log_sigmoid(z) + rest), 0.0)
        outs.append(jnp.einsum('bhqk,bhkd->bhqd', A.astype(vb.dtype), vb))
    out = jnp.concatenate(outs, axis=2)
    return out.transpose(0, 2, 1, 3).reshape(B, T, H * d)


def memory_mixer(q, mem_k, mem_v):
    B, T, H, d = q.shape
    s = jnp.einsum('bthd,bmhd->bhtm', q, mem_k).astype(jnp.float32) * (d ** -0.5)
    p = jax.nn.softmax(s, axis=-1)
    o = jnp.einsum('bhtm,bmhd->bthd', p.astype(mem_v.dtype), mem_v)
    return o.reshape(B, T, H * d)


def setup_inputs(seed: int = 0) -> dict:
    key = jax.random.key(seed)
    ks = jax.random.split(key, 24)
    L, D = DEPTH, D_MODEL

    def w(k, shape, fan_in):
        return jax.random.normal(k, shape, jnp.float32) * (fan_in ** -0.5)

    def gain(k, shape):
        return 1.0 + 0.02 * jax.random.normal(k, shape, jnp.float32)

    return {
        "x": jax.random.normal(ks[0], (BATCH, SEQ, D), jnp.float32),
        "mem": jax.random.normal(ks[1], (BATCH, N_MEM, D), jnp.float32),
        "attn_norm": gain(ks[2], (L, D)),
        "w_in": w(ks[3], (L, D, IN_W), D),
        "gla_w_a2": w(ks[4], (L, GLA_RANK, GLA_HEADS * GLA_DK), GLA_RANK),
        "gla_b_a": 0.1 * jax.random.normal(ks[5], (L, GLA_HEADS * GLA_DK), jnp.float32),
        "gla_out_norm": gain(ks[6], (L, GLA_DV)),
        "w_br_gla": w(ks[7], (L, GLA_HEADS * GLA_DV, D), GLA_HEADS * GLA_DV),
        "sb_q_norm": gain(ks[8], (L, SB_DH)),
        "sb_k_norm": gain(ks[9], (L, SB_DH)),
        "w_br_sb": w(ks[10], (L, SB_HEADS * SB_DH, D), SB_HEADS * SB_DH),
        "mem_norm": gain(ks[11], (L, D)),
        "w_mem_kv": w(ks[12], (L, D, 2 * MEM_HEADS * MEM_DH), D),
        "mem_q_norm": gain(ks[13], (L, MEM_DH)),
        "mem_k_norm": gain(ks[14], (L, MEM_DH)),
        "w_br_mem": w(ks[15], (L, MEM_HEADS * MEM_DH, D), MEM_HEADS * MEM_DH),
        "w_o": w(ks[16], (L, D, D), D),
        "ffn_norm": gain(ks[17], (L, D)),
        "w_gate_up": w(ks[18], (L, D, 2 * D_FF), D),
        "w_down": w(ks[19], (L, D_FF, D), D_FF),
    }


def reference(x, mem, attn_norm, w_in, gla_w_a2, gla_b_a, gla_out_norm, w_br_gla,
              sb_q_norm, sb_k_norm, w_br_sb, mem_norm, w_mem_kv, mem_q_norm, mem_k_norm,
              w_br_mem, w_o, ffn_norm, w_gate_up, w_down):
    B, T, D = x.shape
    M = mem.shape[1]
    for l in range(DEPTH):
        h = rmsnorm(x, attn_norm[l])
        proj = h @ w_in[l]
        (gq, gk, gv, gr, ga1, sq, sk, sv, mq, gates) = jnp.split(proj, SPLIT_POINTS, axis=-1)

        log_a = jax.nn.log_sigmoid((ga1 @ gla_w_a2[l] + gla_b_a[l]).astype(jnp.float32)) / GLA_TAU
        o_gla = gla_mixer(gq.reshape(B, T, GLA_HEADS, GLA_DK), gk.reshape(B, T, GLA_HEADS, GLA_DK),
                          gv.reshape(B, T, GLA_HEADS, GLA_DV), log_a.reshape(B, T, GLA_HEADS, GLA_DK))
        o_gla = rmsnorm(o_gla, gla_out_norm[l]).reshape(B, T, GLA_HEADS * GLA_DV).astype(x.dtype)
        y_gla = (o_gla * jax.nn.silu(gr)) @ w_br_gla[l]

        q_sb = rmsnorm(sq.reshape(B, T, SB_HEADS, SB_DH), sb_q_norm[l])
        k_sb = rmsnorm(sk.reshape(B, T, SB_HEADS, SB_DH), sb_k_norm[l])
        y_sb = stick_breaking_mixer(q_sb, k_sb, sv.reshape(B, T, SB_HEADS, SB_DH)) @ w_br_sb[l]

        mem_kv = rmsnorm(mem, mem_norm[l]) @ w_mem_kv[l]
        m_k, m_v = jnp.split(mem_kv, 2, axis=-1)
        m_k = rmsnorm(m_k.reshape(B, M, MEM_HEADS, MEM_DH), mem_k_norm[l])
        m_v = m_v.reshape(B, M, MEM_HEADS, MEM_DH)
        q_m = rmsnorm(mq.reshape(B, T, MEM_HEADS, MEM_DH), mem_q_norm[l])
        y_mem = memory_mixer(q_m, m_k, m_v) @ w_br_mem[l]

        g = jax.nn.sigmoid(gates.reshape(B, T, N_BRANCH, D))
        merged = g[:, :, 0] * y_gla + g[:, :, 1] * y_sb + g[:, :, 2] * y_mem
        x = x + merged @ w_o[l]

        h2 = rmsnorm(x, ffn_norm[l])
        gate, up = jnp.split(h2 @ w_gate_up[l], 2, axis=-1)
        x = x + (jax.nn.silu(gate) * up) @ w_down[l]
    return x
```

```python
import functools

import jax
import jax.numpy as jnp
from jax import lax
from jax.experimental import pallas as pl
from jax.experimental.pallas import tpu as pltpu

F32 = jnp.float32
BF16 = jnp.bfloat16
EPS = 1e-6

GLA_HEADS = 4
GLA_DK = 128
GLA_DV = 256
GLA_RANK = 16
GLA_TAU = 16.0
GLA_CHUNK = 64
SB_HEADS = 8
SB_DH = 128
MEM_HEADS = 4
MEM_DH = 256
N_BRANCH = 3

LANES = 128
VMEM_LIMIT_BYTES = 56 * 2**20
F32_EXP_UNDERFLOW = -104.0

_NT = (((1,), (1,)), ((), ()))


def _dot(a, b):
    return jnp.dot(a, b, preferred_element_type=F32)


def _neg_softplus(z):
    return jnp.minimum(-z, 0.0) - jnp.log1p(jnp.exp(-jnp.abs(z)))


def _sigmoid(z):
    return 1.0 / (1.0 + jnp.exp(-z))


def _split_bf16(v):
    hi = v.astype(BF16)
    lo = (v - hi.astype(F32)).astype(BF16)
    return hi, lo


def _cparams(*sem):
    return pltpu.CompilerParams(dimension_semantics=sem, vmem_limit_bytes=VMEM_LIMIT_BYTES)


def _rmsnorm_kernel(x_ref, g_ref, o_ref):
    x = x_ref[...]
    ms = jnp.mean(x * x, axis=-1, keepdims=True)
    o_ref[...] = (x * lax.rsqrt(ms + EPS) * g_ref[...]).astype(o_ref.dtype)


def _rmsnorm(x, g, *, tm):
    m, d = x.shape
    return pl.pallas_call(
        _rmsnorm_kernel,
        grid=(m // tm,),
        in_specs=[pl.BlockSpec((tm, d), lambda i: (i, 0)), pl.BlockSpec((1, d), lambda i: (0, 0))],
        out_specs=pl.BlockSpec((tm, d), lambda i: (i, 0)),
        out_shape=jax.ShapeDtypeStruct((m, d), BF16),
        compiler_params=_cparams("parallel"),
        name="rmsnorm",
    )(x, g.reshape(1, d))


def _mm_kernel(*refs, n_a, n_w, epi, group):
    a_refs = refs[:n_a]
    w_refs = refs[n_a:n_a + n_w]
    extra = refs[n_a + n_w:-1]
    o_ref = refs[-1]
    if epi == "swiglu":
        a = a_refs[0][...]
        gate = _dot(a, w_refs[0][...])
        up = _dot(a, w_refs[1][...])
        o_ref[...] = (gate * _sigmoid(gate) * up).astype(o_ref.dtype)
        return
    if epi == "merge":
        out = None
        for a_ref, w_ref, g_ref in zip(a_refs, w_refs, extra):
            y = g_ref[...].astype(F32) * _dot(a_ref[...], w_ref[...])
            out = y if out is None else out + y
        o_ref[...] = out.astype(o_ref.dtype)
        return
    acc = _dot(a_refs[0][...], w_refs[0][...])
    if epi == "cast":
        o_ref[...] = acc.astype(o_ref.dtype)
    elif epi == "sigmoid":
        o_ref[...] = _sigmoid(acc).astype(o_ref.dtype)
    elif epi == "resid":
        o_ref[...] = (extra[0][...] + acc).astype(o_ref.dtype)
    elif epi == "gnorm":
        gain = extra[0][...]
        for c0 in range(0, acc.shape[1], group):
            y = acc[:, c0:c0 + group]
            ms = jnp.mean(y * y, axis=-1, keepdims=True)
            o_ref[:, c0:c0 + group] = (y * lax.rsqrt(ms + EPS) * gain[:, c0:c0 + group]).astype(o_ref.dtype)
    else:
        raise ValueError(epi)


def _matmul(a_list, w_list, w_maps, extras, extra_specs, *, n, tm, tn, out_dtype, epi, name, group=0):
    m = a_list[0].shape[0]
    in_specs = [pl.BlockSpec((tm, a.shape[1]), lambda i, j: (i, 0)) for a in a_list]
    in_specs += [pl.BlockSpec((w.shape[0], tn), wm) for w, wm in zip(w_list, w_maps)]
    in_specs += list(extra_specs)
    kern = functools.partial(_mm_kernel, n_a=len(a_list), n_w=len(w_list), epi=epi, group=group)
    return pl.pallas_call(
        kern,
        grid=(m // tm, n // tn),
        in_specs=in_specs,
        out_specs=pl.BlockSpec((tm, tn), lambda i, j: (i, j)),
        out_shape=jax.ShapeDtypeStruct((m, n), out_dtype),
        compiler_params=_cparams("parallel", "arbitrary"),
        name=name,
    )(*a_list, *w_list, *extras)


def _col(off=0):
    return lambda i, j: (0, j + off)


def _tile(off=0):
    return lambda i, j: (i, j + off)


def _gla_kernel(q_ref, k_ref, v_ref, r_ref, a1_ref, wa2_ref, ba_ref, gn_ref, o_ref, s_ref, *, bt):
    c_len = GLA_CHUNK

    @pl.when(pl.program_id(0) == 0)
    def _():
        s_ref[...] = jnp.zeros_like(s_ref)

    row = lax.broadcasted_iota(jnp.int32, (bt, bt), 0)
    col = lax.broadcasted_iota(jnp.int32, (bt, bt), 1)
    shift = c_len.bit_length() - 1
    same_chunk = (row >> shift) == (col >> shift)
    tri = jnp.logical_and(col <= row, same_chunk).astype(BF16)
    crow = lax.broadcasted_iota(jnp.int32, (c_len, c_len), 0)
    ccol = lax.broadcasted_iota(jnp.int32, (c_len, c_len), 1)
    causal = ccol <= crow

    pre = _dot(a1_ref[...], wa2_ref[...]) + ba_ref[...]
    log_a = _neg_softplus(-pre) * (1.0 / GLA_TAU)
    la_hi, la_lo = _split_bf16(log_a)
    b = _dot(tri, la_hi) + _dot(tri, la_lo)

    gn = gn_ref[...]
    scale = GLA_DK ** -0.5
    for c in range(bt // c_len):
        rs = slice(c * c_len, (c + 1) * c_len)
        for h in range(GLA_HEADS):
            ks = slice(h * GLA_DK, (h + 1) * GLA_DK)
            vs = slice(h * GLA_DV, (h + 1) * GLA_DV)
            bh = b[rs, ks]
            bt_ = bh.T
            b_last = bt_[:, c_len - 1:c_len]
            q = q_ref[rs, ks] * scale
            k_t = k_ref[rs, ks].T
            v = v_ref[rs, vs].astype(BF16)
            q_e = (q * jnp.exp(bh)).astype(BF16)
            k_e = (k_t * jnp.exp(-bt_)).astype(BF16)
            k_d = (k_t * jnp.exp(b_last - bt_)).astype(BF16)
            s = jnp.where(causal, _dot(q_e, k_e), 0.0).astype(BF16)
            state = s_ref[h]
            o = _dot(s, v) + _dot(q_e, state.astype(BF16))
            s_ref[h] = state * jnp.exp(b_last) + _dot(k_d, v)
            ms = jnp.mean(o * o, axis=-1, keepdims=True)
            o_n = o * lax.rsqrt(ms + EPS) * gn
            r = r_ref[rs, vs]
            o_ref[rs, vs] = (o_n * (r * _sigmoid(r))).astype(o_ref.dtype)


def _gla(p_gla, a1, wa2p, ba, gn, *, bt):
    t = p_gla.shape[0]
    hk = GLA_HEADS * GLA_DK
    hv = GLA_HEADS * GLA_DV
    return pl.pallas_call(
        functools.partial(_gla_kernel, bt=bt),
        grid=(t // bt,),
        in_specs=[
            pl.BlockSpec((bt, hk), lambda i: (i, 0)),
            pl.BlockSpec((bt, hk), lambda i: (i, 1)),
            pl.BlockSpec((bt, hv), lambda i: (i, 1)),
            pl.BlockSpec((bt, hv), lambda i: (i, 2)),
            pl.BlockSpec((bt, LANES), lambda i: (i, 0)),
            pl.BlockSpec((LANES, hk), lambda i: (0, 0)),
            pl.BlockSpec((1, hk), lambda i: (0, 0)),
            pl.BlockSpec((1, GLA_DV), lambda i: (0, 0)),
        ],
        out_specs=pl.BlockSpec((bt, hv), lambda i: (i, 0)),
        out_shape=jax.ShapeDtypeStruct((t, hv), BF16),
        scratch_shapes=[pltpu.VMEM((GLA_HEADS, GLA_DK, GLA_DV), F32)],
        compiler_params=_cparams("arbitrary"),
        name="gla",
    )(p_gla, p_gla, p_gla, p_gla, a1, wa2p, ba.reshape(1, hk), gn.reshape(1, GLA_DV))


def _sb_kernel(q_ref, k_ref, v_ref, o_ref, acc_ref, c_ref, *, tile, scale):
    i = pl.program_id(1)
    q = q_ref[...]
    row = lax.broadcasted_iota(jnp.int32, (tile, tile), 0)
    col = lax.broadcasted_iota(jnp.int32, (tile, tile), 1)
    later = (row > col).astype(BF16)
    causal = col < row

    def step(j, diag):
        start = pl.multiple_of(j * tile, tile)
        k = k_ref[pl.ds(start, tile), :]
        v = v_ref[pl.ds(start, tile), :]
        z = lax.dot_general(q, k, _NT, preferred_element_type=F32) * scale
        l = _neg_softplus(z)
        if diag:
            l = jnp.where(causal, l, 0.0)
        l_hi, l_lo = _split_bf16(l)
        within = _dot(l_hi, later) + _dot(l_lo, later)
        c = c_ref[...]
        a = jnp.exp(z + l + within + c)
        if diag:
            a = jnp.where(causal, a, 0.0)
        acc_ref[...] += _dot(a.astype(BF16), v)
        c_new = c + jnp.sum(l, axis=-1, keepdims=True)
        c_ref[...] = c_new
        return jnp.max(c_new)

    acc_ref[...] = jnp.zeros_like(acc_ref)
    c_ref[...] = jnp.zeros_like(c_ref)
    c_max = step(i, True)

    def cond(carry):
        j, c_max = carry
        return jnp.logical_and(j >= 0, c_max >= F32_EXP_UNDERFLOW)

    def body(carry):
        j, _ = carry
        return j - 1, step(j, False)

    lax.while_loop(cond, body, (i - 1, c_max))
    o_ref[...] = acc_ref[...].astype(o_ref.dtype)


def _sb_attention(qk, v, *, tile):
    t = v.shape[0]
    return pl.pallas_call(
        functools.partial(_sb_kernel, tile=tile, scale=SB_DH ** -0.5),
        grid=(SB_HEADS, t // tile),
        in_specs=[
            pl.BlockSpec((tile, SB_DH), lambda h, i: (i, h)),
            pl.BlockSpec((t, SB_DH), lambda h, i: (0, SB_HEADS + h)),
            pl.BlockSpec((t, SB_DH), lambda h, i: (0, h)),
        ],
        out_specs=pl.BlockSpec((tile, SB_DH), lambda h, i: (i, h)),
        out_shape=jax.ShapeDtypeStruct((t, SB_HEADS * SB_DH), BF16),
        scratch_shapes=[pltpu.VMEM((tile, SB_DH), F32), pltpu.VMEM((tile, 1), F32)],
        compiler_params=_cparams("parallel", "parallel"),
        name="sb_attention",
    )(qk, qk, v)


def _mem_kernel(q_ref, k_ref, v_ref, o_ref):
    scale = MEM_DH ** -0.5
    for h in range(MEM_HEADS):
        cs = slice(h * MEM_DH, (h + 1) * MEM_DH)
        s = lax.dot_general(q_ref[:, cs], k_ref[:, cs], _NT, preferred_element_type=F32) * scale
        e = jnp.exp(s - jnp.max(s, axis=-1, keepdims=True))
        p = e / jnp.sum(e, axis=-1, keepdims=True)
        o_ref[:, cs] = _dot(p.astype(BF16), v_ref[:, cs]).astype(o_ref.dtype)


def _mem_attention(q, mk, mv, *, tq):
    t, w = q.shape
    m = mk.shape[0]
    return pl.pallas_call(
        _mem_kernel,
        grid=(t // tq,),
        in_specs=[
            pl.BlockSpec((tq, w), lambda i: (i, 0)),
            pl.BlockSpec((m, w), lambda i: (0, 0)),
            pl.BlockSpec((m, w), lambda i: (0, 0)),
        ],
        out_specs=pl.BlockSpec((tq, w), lambda i: (i, 0)),
        out_shape=jax.ShapeDtypeStruct((t, w), BF16),
        compiler_params=_cparams("parallel"),
        name="mem_attention",
    )(q, mk, mv)


def _layer(x, mem, p):
    t, d = x.shape
    m = mem.shape[0]
    hk = GLA_HEADS * GLA_DK
    hv = GLA_HEADS * GLA_DV
    mix = SB_HEADS * SB_DH
    d_ff = p["w_down"].shape[0]

    sizes = [hk, hk, hv, hv, GLA_RANK, mix, mix, mix, MEM_HEADS * MEM_DH, N_BRANCH * d]
    offs = [0]
    for s in sizes:
        offs.append(offs[-1] + s)
    w_in = p["w_in"]
    w_gla = w_in[:, offs[0]:offs[4]].astype(BF16)
    w_a1 = jnp.pad(w_in[:, offs[4]:offs[5]], ((0, 0), (0, LANES - GLA_RANK))).astype(BF16)
    w_sqk = w_in[:, offs[5]:offs[7]].astype(BF16)
    w_sv = w_in[:, offs[7]:offs[8]].astype(BF16)
    w_mq = w_in[:, offs[8]:offs[9]].astype(BF16)
    w_gates = w_in[:, offs[9]:offs[10]].astype(BF16)
    wa2p = jnp.pad(p["gla_w_a2"], ((0, LANES - GLA_RANK), (0, 0))).astype(BF16)

    h = _rmsnorm(x, p["attn_norm"], tm=512)

    one = lambda a, w, **kw: _matmul([a], [w], [_col()], kw.pop("extras", []), kw.pop("extra_specs", []), **kw)
    gain_spec = lambda tn: pl.BlockSpec((1, tn), lambda i, j: (0, j))

    p_gla = one(h, w_gla, n=offs[4], tm=1024, tn=1024, out_dtype=F32, epi="cast", name="proj_gla")
    a1 = one(h, w_a1, n=LANES, tm=1024, tn=LANES, out_dtype=BF16, epi="cast", name="proj_a1")
    o_gla = _gla(p_gla, a1, wa2p, p["gla_b_a"], p["gla_out_norm"], bt=256)

    qk_gain = jnp.concatenate([jnp.tile(p["sb_q_norm"], SB_HEADS), jnp.tile(p["sb_k_norm"], SB_HEADS)]).reshape(1, -1)
    qk = one(h, w_sqk, n=2 * mix, tm=1024, tn=1024, out_dtype=BF16, epi="gnorm", group=SB_DH, name="proj_sb_qk",
             extras=[qk_gain], extra_specs=[gain_spec(1024)])
    sv = one(h, w_sv, n=mix, tm=1024, tn=1024, out_dtype=BF16, epi="cast", name="proj_sb_v")
    o_sb = _sb_attention(qk, sv, tile=256)

    mw = MEM_HEADS * MEM_DH
    hm = _rmsnorm(mem, p["mem_norm"], tm=m)
    w_mkv = p["w_mem_kv"].astype(BF16)
    mk_gain = jnp.tile(p["mem_k_norm"], MEM_HEADS).reshape(1, -1)
    mq_gain = jnp.tile(p["mem_q_norm"], MEM_HEADS).reshape(1, -1)
    m_k = _matmul([hm], [w_mkv], [_col()], [mk_gain], [gain_spec(mw)], n=mw, tm=m, tn=mw, out_dtype=BF16,
                  epi="gnorm", group=MEM_DH, name="proj_mem_k")
    m_v = _matmul([hm], [w_mkv], [_col(1)], [], [], n=mw, tm=m, tn=mw, out_dtype=BF16, epi="cast", name="proj_mem_v")
    q_m = one(h, w_mq, n=mw, tm=1024, tn=1024, out_dtype=BF16, epi="gnorm", group=MEM_DH, name="proj_mem_q",
              extras=[mq_gain], extra_specs=[gain_spec(1024)])
    o_mem = _mem_attention(q_m, m_k, m_v, tq=512)

    gates = one(h, w_gates, n=N_BRANCH * d, tm=1024, tn=1024, out_dtype=BF16, epi="sigmoid", name="proj_gates")
    tn = 1024
    nj = d // tn
    g_spec = lambda b: pl.BlockSpec((512, tn), _tile(b * nj))
    merged = _matmul(
        [o_gla, o_sb, o_mem],
        [p["w_br_gla"].astype(BF16), p["w_br_sb"].astype(BF16), p["w_br_mem"].astype(BF16)],
        [_col(), _col(), _col()],
        [gates, gates, gates], [g_spec(0), g_spec(1), g_spec(2)],
        n=d, tm=512, tn=tn, out_dtype=BF16, epi="merge", name="branch_merge")
    x = one(merged, p["w_o"].astype(BF16), n=d, tm=1024, tn=1024, out_dtype=F32, epi="resid", name="out_proj",
            extras=[x], extra_specs=[pl.BlockSpec((1024, 1024), _tile())])

    h2 = _rmsnorm(x, p["ffn_norm"], tm=512)
    w_gu = p["w_gate_up"].astype(BF16)
    tn = 512
    act = _matmul([h2], [w_gu, w_gu], [_col(), _col(d_ff // tn)], [], [], n=d_ff, tm=1024, tn=tn, out_dtype=BF16,
                  epi="swiglu", name="ffn_gate_up")
    x = one(act, p["w_down"].astype(BF16), n=d, tm=512, tn=1024, out_dtype=F32, epi="resid", name="ffn_down",
            extras=[x], extra_specs=[pl.BlockSpec((512, 1024), _tile())])
    return x


_PARAM_NAMES = ("attn_norm", "w_in", "gla_w_a2", "gla_b_a", "gla_out_norm", "w_br_gla", "sb_q_norm", "sb_k_norm",
                "w_br_sb", "mem_norm", "w_mem_kv", "mem_q_norm", "mem_k_norm", "w_br_mem", "w_o", "ffn_norm",
                "w_gate_up", "w_down")


def kernel(x, mem, attn_norm, w_in, gla_w_a2, gla_b_a, gla_out_norm, w_br_gla, sb_q_norm, sb_k_norm, w_br_sb,
           mem_norm, w_mem_kv, mem_q_norm, mem_k_norm, w_br_mem, w_o, ffn_norm, w_gate_up, w_down):
    params = (attn_norm, w_in, gla_w_a2, gla_b_a, gla_out_norm, w_br_gla, sb_q_norm, sb_k_norm, w_br_sb,
              mem_norm, w_mem_kv, mem_q_norm, mem_k_norm, w_br_mem, w_o, ffn_norm, w_gate_up, w_down)
    b, t, d = x.shape
    assert b == 1, "kernels are written for a single sequence"
    xs = x.reshape(t, d)
    ms = mem.reshape(mem.shape[1], d)
    for layer in range(w_in.shape[0]):
        xs = _layer(xs, ms, {n: v[layer] for n, v in zip(_PARAM_NAMES, params)})
    return xs.reshape(b, t, d)
```

```python
import functools
import math

import jax
import jax.numpy as jnp
from jax import lax
from jax.experimental import pallas as pl
from jax.experimental.pallas import tpu as pltpu

F32 = jnp.float32
BF16 = jnp.bfloat16
EPS = 1e-6

GLA_HEADS = 4
GLA_DK = 128
GLA_DV = 256
GLA_RANK = 16
GLA_TAU = 16.0
GLA_CHUNK = 64
SB_HEADS = 8
SB_DH = 128
MEM_HEADS = 4
MEM_DH = 256
N_BRANCH = 3

LANES = 128
VMEM_LIMIT_BYTES = 56 * 2**20
F32_EXP2_UNDERFLOW = 150.0
LOG2_E = math.log2(math.e)

_NT = (((1,), (1,)), ((), ()))


def _dot(a, b):
    return jnp.dot(a, b, preferred_element_type=F32)


def _neg_softplus(z):
    return jnp.minimum(-z, 0.0) - jnp.log1p(jnp.exp(-jnp.abs(z)))


def _softplus2(z2):
    return jnp.where(z2 > 64.0, z2, jnp.log2(1.0 + jnp.exp2(z2)))


def _sigmoid(z):
    return 1.0 / (1.0 + jnp.exp(-z))


def _split_bf16(v):
    hi = v.astype(BF16)
    lo = (v - hi.astype(F32)).astype(BF16)
    return hi, lo


def _cparams(*sem):
    return pltpu.CompilerParams(dimension_semantics=sem, vmem_limit_bytes=VMEM_LIMIT_BYTES)


def _rmsnorm_kernel(x_ref, g_ref, o_ref):
    x = x_ref[...]
    ms = jnp.mean(x * x, axis=-1, keepdims=True)
    o_ref[...] = (x * lax.rsqrt(ms + EPS) * g_ref[...]).astype(o_ref.dtype)


def _rmsnorm(x, g, *, tm):
    m, d = x.shape
    return pl.pallas_call(
        _rmsnorm_kernel,
        grid=(m // tm,),
        in_specs=[pl.BlockSpec((tm, d), lambda i: (i, 0)), pl.BlockSpec((1, d), lambda i: (0, 0))],
        out_specs=pl.BlockSpec((tm, d), lambda i: (i, 0)),
        out_shape=jax.ShapeDtypeStruct((m, d), BF16),
        compiler_params=_cparams("parallel"),
        name="rmsnorm",
    )(x, g.reshape(1, d))


def _mm_body(a_refs, w_refs, extra, o_ref, epi, group):
    if epi == "swiglu":
        a = a_refs[0][...]
        gate = _dot(a, w_refs[0][...])
        up = _dot(a, w_refs[1][...])
        o_ref[...] = (gate * _sigmoid(gate) * up).astype(o_ref.dtype)
        return
    if epi == "merge":
        out = None
        for a_ref, w_ref, g_ref in zip(a_refs, w_refs, extra):
            y = g_ref[...].astype(F32) * _dot(a_ref[...], w_ref[...])
            out = y if out is None else out + y
        o_ref[...] = out.astype(o_ref.dtype)
        return
    acc = _dot(a_refs[0][...], w_refs[0][...])
    if epi == "cast":
        o_ref[...] = acc.astype(o_ref.dtype)
    elif epi == "sigmoid":
        o_ref[...] = _sigmoid(acc).astype(o_ref.dtype)
    elif epi == "resid":
        o_ref[...] = (extra[0][...] + acc).astype(o_ref.dtype)
    elif epi == "gnorm":
        gain = extra[0][...]
        for c0 in range(0, acc.shape[1], group):
            y = acc[:, c0:c0 + group]
            ms = jnp.mean(y * y, axis=-1, keepdims=True)
            o_ref[:, c0:c0 + group] = (y * lax.rsqrt(ms + EPS) * gain[:, c0:c0 + group]).astype(o_ref.dtype)
    else:
        raise ValueError(epi)


def _mm_kernel(*refs, n_a, n_w, epi, group):
    _mm_body(refs[:n_a], refs[n_a:n_a + n_w], refs[n_a + n_w:-1], refs[-1], epi, group)


def _mm_wcast_kernel(*refs, n_a, n_w, epi, group):
    a_refs = refs[:n_a]
    w_refs = refs[n_a:n_a + n_w]
    extra = refs[n_a + n_w:-1 - n_w]
    o_ref = refs[-1 - n_w]
    wbf_refs = refs[-n_w:]

    @pl.when(pl.program_id(1) == 0)
    def _():
        for w_ref, wbf_ref in zip(w_refs, wbf_refs):
            wbf_ref[...] = w_ref[...].astype(BF16)

    _mm_body(a_refs, wbf_refs, extra, o_ref, epi, group)


def _matmul(a_list, w_list, w_maps, extras, extra_specs, *, n, tm, tn, out_dtype, epi, name, group=0):
    m = a_list[0].shape[0]
    in_specs = [pl.BlockSpec((tm, a.shape[1]), lambda i, j: (i, 0)) for a in a_list]
    in_specs += [pl.BlockSpec((None, w.shape[1], tn), wm) for w, wm in zip(w_list, w_maps)]
    in_specs += list(extra_specs)
    kern = functools.partial(_mm_kernel, n_a=len(a_list), n_w=len(w_list), epi=epi, group=group)
    return pl.pallas_call(
        kern,
        grid=(m // tm, n // tn),
        in_specs=in_specs,
        out_specs=pl.BlockSpec((tm, tn), lambda i, j: (i, j)),
        out_shape=jax.ShapeDtypeStruct((m, n), out_dtype),
        compiler_params=_cparams("parallel", "arbitrary"),
        name=name,
    )(*a_list, *w_list, *extras)


def _matmul_wcast(a_list, w_list, w_maps, extras, extra_specs, *, n, tm, tn, out_dtype, epi, name, group=0):
    m = a_list[0].shape[0]
    in_specs = [pl.BlockSpec((tm, a.shape[1]), lambda j, i: (i, 0)) for a in a_list]
    in_specs += [pl.BlockSpec((None, w.shape[1], tn), wm) for w, wm in zip(w_list, w_maps)]
    in_specs += list(extra_specs)
    kern = functools.partial(_mm_wcast_kernel, n_a=len(a_list), n_w=len(w_list), epi=epi, group=group)
    return pl.pallas_call(
        kern,
        grid=(n // tn, m // tm),
        in_specs=in_specs,
        out_specs=pl.BlockSpec((tm, tn), lambda j, i: (i, j)),
        out_shape=jax.ShapeDtypeStruct((m, n), out_dtype),
        scratch_shapes=[pltpu.VMEM((w.shape[1], tn), BF16) for w in w_list],
        compiler_params=_cparams("parallel", "arbitrary"),
        name=name,
    )(*a_list, *w_list, *extras)


def _gla_kernel(q_ref, k_ref, v_ref, r_ref, a1_ref, wa2_ref, ba_ref, gn_ref, o_ref, s_ref, *, bt):
    c_len = GLA_CHUNK

    @pl.when(pl.program_id(0) == 0)
    def _():
        s_ref[...] = jnp.zeros_like(s_ref)

    row = lax.broadcasted_iota(jnp.int32, (bt, bt), 0)
    col = lax.broadcasted_iota(jnp.int32, (bt, bt), 1)
    shift = c_len.bit_length() - 1
    same_chunk = (row >> shift) == (col >> shift)
    tri = jnp.logical_and(col <= row, same_chunk).astype(BF16)
    crow = lax.broadcasted_iota(jnp.int32, (c_len, c_len), 0)
    ccol = lax.broadcasted_iota(jnp.int32, (c_len, c_len), 1)
    causal = ccol <= crow

    pre = _dot(a1_ref[...], wa2_ref[...]) + ba_ref[...]
    log_a = _neg_softplus(-pre) * (1.0 / GLA_TAU)
    la_hi, la_lo = _split_bf16(log_a)
    b = _dot(tri, la_hi) + _dot(tri, la_lo)

    gn = gn_ref[...]
    scale = GLA_DK ** -0.5
    for c in range(bt // c_len):
        rs = slice(c * c_len, (c + 1) * c_len)
        for h in range(GLA_HEADS):
            ks = slice(h * GLA_DK, (h + 1) * GLA_DK)
            vs = slice(h * GLA_DV, (h + 1) * GLA_DV)
            bh = b[rs, ks]
            bt_ = bh.T
            b_last = bt_[:, c_len - 1:c_len]
            q = q_ref[rs, ks] * scale
            k_t = k_ref[rs, ks].T
            v = v_ref[rs, vs].astype(BF16)
            q_e = (q * jnp.exp(bh)).astype(BF16)
            k_e = (k_t * jnp.exp(-bt_)).astype(BF16)
            k_d = (k_t * jnp.exp(b_last - bt_)).astype(BF16)
            s = jnp.where(causal, _dot(q_e, k_e), 0.0).astype(BF16)
            state = s_ref[h]
            o = _dot(s, v) + _dot(q_e, state.astype(BF16))
            s_ref[h] = state * jnp.exp(b_last) + _dot(k_d, v)
            ms = jnp.mean(o * o, axis=-1, keepdims=True)
            o_n = o * lax.rsqrt(ms + EPS) * gn
            r = r_ref[rs, vs]
            o_ref[rs, vs] = (o_n * (r * _sigmoid(r))).astype(o_ref.dtype)


def _gla(p_gla, a1, wa2p, ba, gn, *, bt):
    t = p_gla.shape[0]
    hk = GLA_HEADS * GLA_DK
    hv = GLA_HEADS * GLA_DV
    return pl.pallas_call(
        functools.partial(_gla_kernel, bt=bt),
        grid=(t // bt,),
        in_specs=[
            pl.BlockSpec((bt, hk), lambda i: (i, 0)),
            pl.BlockSpec((bt, hk), lambda i: (i, 1)),
            pl.BlockSpec((bt, hv), lambda i: (i, 1)),
            pl.BlockSpec((bt, hv), lambda i: (i, 2)),
            pl.BlockSpec((bt, LANES), lambda i: (i, 0)),
            pl.BlockSpec((LANES, hk), lambda i: (0, 0)),
            pl.BlockSpec((1, hk), lambda i: (0, 0)),
            pl.BlockSpec((1, GLA_DV), lambda i: (0, 0)),
        ],
        out_specs=pl.BlockSpec((bt, hv), lambda i: (i, 0)),
        out_shape=jax.ShapeDtypeStruct((t, hv), BF16),
        scratch_shapes=[pltpu.VMEM((GLA_HEADS, GLA_DK, GLA_DV), F32)],
        compiler_params=_cparams("arbitrary"),
        name="gla",
    )(p_gla, p_gla, p_gla, p_gla, a1, wa2p, ba.reshape(1, hk), gn.reshape(1, GLA_DV))


def _sb_kernel(q_ref, k_ref, v_ref, o_ref, *, tile, heads):
    i = pl.program_id(1)
    row = lax.broadcasted_iota(jnp.int32, (tile, tile), 0)
    col = lax.broadcasted_iota(jnp.int32, (tile, tile), 1)
    from_here = (row >= col).astype(BF16)

    def step_head(hd, j, c, acc, diag):
        cs = slice(hd * SB_DH, (hd + 1) * SB_DH)
        start = pl.multiple_of(j * tile, tile)
        q = q_ref[:, cs]
        k = k_ref[pl.ds(start, tile), cs]
        v = v_ref[pl.ds(start, tile), cs]
        z = lax.dot_general(q, k, _NT, preferred_element_type=F32)
        l = _softplus2(z)
        if diag:
            causal = col < row
            l = jnp.where(causal, l, 0.0)
        e = z - _dot(l.astype(BF16), from_here)
        if c is not None:
            e = e - c
        a = jnp.exp2(e)
        if diag:
            a = jnp.where(causal, a, 0.0)
        pv = _dot(a.astype(BF16), v)
        tot = jnp.sum(l, axis=-1, keepdims=True)
        if c is None:
            return tot, pv
        return c + tot, acc + pv

    def step(j, state, diag):
        if state is None:
            state = [(None, None)] * heads
        return tuple(step_head(hd, j, c, acc, diag) for hd, (c, acc) in enumerate(state))

    def only_diag():
        return step(i, None, True)

    def diag_and_previous():
        return step(i - 1, step(i, None, True), False)

    state = lax.cond(i == 0, only_diag, diag_and_previous)

    def cond(carry):
        j, state = carry
        c_min = functools.reduce(jnp.minimum, [jnp.min(c) for c, _ in state])
        return jnp.logical_and(j >= 0, c_min <= F32_EXP2_UNDERFLOW)

    def body(carry):
        j, state = carry
        return j - 1, step(j, state, False)

    _, state = lax.while_loop(cond, body, (i - 2, state))
    for hd, (_, acc) in enumerate(state):
        o_ref[:, hd * SB_DH:(hd + 1) * SB_DH] = acc.astype(o_ref.dtype)


def _sb_attention(qk, v, *, tile, heads):
    t = v.shape[0]
    groups = SB_HEADS // heads
    w = heads * SB_DH
    return pl.pallas_call(
        functools.partial(_sb_kernel, tile=tile, heads=heads),
        grid=(groups, t // tile),
        in_specs=[
            pl.BlockSpec((tile, w), lambda g, i: (i, g)),
            pl.BlockSpec((t, w), lambda g, i: (0, groups + g)),
            pl.BlockSpec((t, w), lambda g, i: (0, g)),
        ],
        out_specs=pl.BlockSpec((tile, w), lambda g, i: (i, g)),
        out_shape=jax.ShapeDtypeStruct((t, SB_HEADS * SB_DH), BF16),
        compiler_params=_cparams("parallel", "parallel"),
        name="sb_attention",
    )(qk, qk, v)


def _mem_kernel(q_ref, k_ref, v_ref, o_ref):
    scale = MEM_DH ** -0.5
    for h in range(MEM_HEADS):
        cs = slice(h * MEM_DH, (h + 1) * MEM_DH)
        s = lax.dot_general(q_ref[:, cs], k_ref[:, cs], _NT, preferred_element_type=F32) * scale
        e = jnp.exp(s - jnp.max(s, axis=-1, keepdims=True))
        p = e / jnp.sum(e, axis=-1, keepdims=True)
        o_ref[:, cs] = _dot(p.astype(BF16), v_ref[:, cs]).astype(o_ref.dtype)


def _mem_attention(q, mk, mv, *, tq):
    t, w = q.shape
    m = mk.shape[0]
    return pl.pallas_call(
        _mem_kernel,
        grid=(t // tq,),
        in_specs=[
            pl.BlockSpec((tq, w), lambda i: (i, 0)),
            pl.BlockSpec((m, w), lambda i: (0, 0)),
            pl.BlockSpec((m, w), lambda i: (0, 0)),
        ],
        out_specs=pl.BlockSpec((tq, w), lambda i: (i, 0)),
        out_shape=jax.ShapeDtypeStruct((t, w), BF16),
        compiler_params=_cparams("parallel"),
        name="mem_attention",
    )(q, mk, mv)


PROJ_TN = 1024


def _pack_w_in(w_in, d):
    hk = GLA_HEADS * GLA_DK
    hv = GLA_HEADS * GLA_DV
    mix = SB_HEADS * SB_DH
    sizes = {"gla": 2 * hk + 2 * hv, "a1": GLA_RANK, "sb_qk": 2 * mix, "sb_v": mix, "mem_q": MEM_HEADS * MEM_DH,
             "gates": N_BRANCH * d}
    pieces, tiles, src, dst = [], {}, 0, 0
    for name, size in sizes.items():
        tiles[name] = dst // PROJ_TN
        pieces.append(w_in[:, :, src:src + size].astype(BF16))
        pad = -size % PROJ_TN
        if pad:
            pieces.append(jnp.zeros(w_in.shape[:2] + (pad,), BF16))
        src += size
        dst += size + pad
    return jnp.concatenate(pieces, axis=-1), tiles


def _layer(x, mem, p, w_in_packed, seg, layer):
    t, d = x.shape
    m = mem.shape[0]
    hk = GLA_HEADS * GLA_DK
    hv = GLA_HEADS * GLA_DV
    mix = SB_HEADS * SB_DH
    mw = MEM_HEADS * MEM_DH
    d_ff = p["w_down"].shape[1]

    def wcol(off=0):
        return lambda i, j: (layer, 0, j + off)

    def wcol_t(off=0):
        return lambda j, i: (layer, 0, j + off)

    def proj(a, name, n, out_dtype, epi, *, tn=PROJ_TN, group=0, extras=(), extra_specs=()):
        off = seg[name] * (PROJ_TN // tn)
        return _matmul([a], [w_in_packed], [wcol(off)], list(extras), list(extra_specs), n=n, tm=1024, tn=tn,
                       out_dtype=out_dtype, epi=epi, group=group, name="proj_" + name)

    gain_spec = pl.BlockSpec((1, PROJ_TN), lambda i, j: (0, j))
    gain_spec_t = pl.BlockSpec((1, mw), lambda j, i: (0, j))

    h = _rmsnorm(x, p["attn_norm"][layer], tm=512)

    p_gla = proj(h, "gla", 2 * hk + 2 * hv, F32, "cast")
    a1 = proj(h, "a1", LANES, BF16, "cast", tn=LANES)
    wa2p = jnp.pad(p["gla_w_a2"][layer], ((0, LANES - GLA_RANK), (0, 0))).astype(BF16)
    o_gla = _gla(p_gla, a1, wa2p, p["gla_b_a"][layer], p["gla_out_norm"][layer], bt=256)

    q_gain = p["sb_q_norm"][layer] * (SB_DH ** -0.5 * LOG2_E)
    qk_gain = jnp.concatenate([jnp.tile(q_gain, SB_HEADS), jnp.tile(p["sb_k_norm"][layer], SB_HEADS)]).reshape(1, -1)
    qk = proj(h, "sb_qk", 2 * mix, BF16, "gnorm", group=SB_DH, extras=[qk_gain], extra_specs=[gain_spec])
    sv = proj(h, "sb_v", mix, BF16, "cast")
    o_sb = _sb_attention(qk, sv, tile=256, heads=2)

    hm = _rmsnorm(mem, p["mem_norm"][layer], tm=m)
    mk_gain = jnp.tile(p["mem_k_norm"][layer], MEM_HEADS).reshape(1, -1)
    mq_gain = jnp.tile(p["mem_q_norm"][layer], MEM_HEADS).reshape(1, -1)
    m_k = _matmul_wcast([hm], [p["w_mem_kv"]], [wcol_t()], [mk_gain], [gain_spec_t], n=mw, tm=m, tn=mw,
                        out_dtype=BF16, epi="gnorm", group=MEM_DH, name="proj_mem_k")
    m_v = _matmul_wcast([hm], [p["w_mem_kv"]], [wcol_t(1)], [], [], n=mw, tm=m, tn=mw, out_dtype=BF16, epi="cast",
                        name="proj_mem_v")
    q_m = proj(h, "mem_q", mw, BF16, "gnorm", group=MEM_DH, extras=[mq_gain], extra_specs=[gain_spec])
    o_mem = _mem_attention(q_m, m_k, m_v, tq=512)

    gates = proj(h, "gates", N_BRANCH * d, BF16, "sigmoid")
    tm, tn = 1024, 512
    nj = d // tn
    g_spec = lambda b: pl.BlockSpec((tm, tn), lambda j, i: (i, j + b * nj))
    merged = _matmul_wcast(
        [o_gla, o_sb, o_mem], [p["w_br_gla"], p["w_br_sb"], p["w_br_mem"]], [wcol_t(), wcol_t(), wcol_t()],
        [gates, gates, gates], [g_spec(0), g_spec(1), g_spec(2)],
        n=d, tm=tm, tn=tn, out_dtype=BF16, epi="merge", name="branch_merge")
    x = _matmul_wcast([merged], [p["w_o"]], [wcol_t()], [x], [pl.BlockSpec((tm, tn), lambda j, i: (i, j))],
                      n=d, tm=tm, tn=tn, out_dtype=F32, epi="resid", name="out_proj")

    h2 = _rmsnorm(x, p["ffn_norm"][layer], tm=512)
    act = _matmul_wcast([h2], [p["w_gate_up"], p["w_gate_up"]], [wcol_t(), wcol_t(d_ff // tn)], [], [],
                        n=d_ff, tm=tm, tn=tn, out_dtype=BF16, epi="swiglu", name="ffn_gate_up")
    tm = 512
    x = _matmul_wcast([act], [p["w_down"]], [wcol_t()], [x], [pl.BlockSpec((tm, tn), lambda j, i: (i, j))],
                      n=d, tm=tm, tn=tn, out_dtype=F32, epi="resid", name="ffn_down")
    return x


_PARAM_NAMES = ("attn_norm", "w_in", "gla_w_a2", "gla_b_a", "gla_out_norm", "w_br_gla", "sb_q_norm", "sb_k_norm",
                "w_br_sb", "mem_norm", "w_mem_kv", "mem_q_norm", "mem_k_norm", "w_br_mem", "w_o", "ffn_norm",
                "w_gate_up", "w_down")


def kernel(x, mem, attn_norm, w_in, gla_w_a2, gla_b_a, gla_out_norm, w_br_gla, sb_q_norm, sb_k_norm, w_br_sb,
           mem_norm, w_mem_kv, mem_q_norm, mem_k_norm, w_br_mem, w_o, ffn_norm, w_gate_up, w_down):
    params = dict(zip(_PARAM_NAMES, (attn_norm, w_in, gla_w_a2, gla_b_a, gla_out_norm, w_br_gla, sb_q_norm,
                                     sb_k_norm, w_br_sb, mem_norm, w_mem_kv, mem_q_norm, mem_k_norm, w_br_mem,
                                     w_o, ffn_norm, w_gate_up, w_down)))
    b, t, d = x.shape
    assert b == 1, "kernels are written for a single sequence"
    xs = x.reshape(t, d)
    ms = mem.reshape(mem.shape[1], d)
    w_in_packed, seg = _pack_w_in(w_in, d)
    for layer in range(w_in.shape[0]):
        xs = _layer(xs, ms, params, w_in_packed, seg, layer)
    return xs.reshape(b, t, d)
```

```python
import functools
import math

import jax
import jax.numpy as jnp
from jax import lax
from jax.experimental import pallas as pl
from jax.experimental.pallas import tpu as pltpu

F32 = jnp.float32
BF16 = jnp.bfloat16
EPS = 1e-6

GLA_HEADS = 4
GLA_DK = 128
GLA_DV = 256
GLA_RANK = 16
GLA_TAU = 16.0
GLA_CHUNK = 64
SB_HEADS = 8
SB_DH = 128
MEM_HEADS = 4
MEM_DH = 256
N_BRANCH = 3

LANES = 128
VMEM_LIMIT_BYTES = 56 * 2**20
F32_EXP2_UNDERFLOW = 150.0
LOG2_E = math.log2(math.e)

_NT = (((1,), (1,)), ((), ()))


def _dot(a, b):
    return jnp.dot(a, b, preferred_element_type=F32)


def _neg_softplus(z):
    return jnp.minimum(-z, 0.0) - jnp.log1p(jnp.exp(-jnp.abs(z)))


def _softplus2(z2):
    return jnp.where(z2 > 64.0, z2, jnp.log2(1.0 + jnp.exp2(z2)))


def _sigmoid(z):
    return 1.0 / (1.0 + jnp.exp(-z))


def _split_bf16(v):
    hi = v.astype(BF16)
    lo = (v - hi.astype(F32)).astype(BF16)
    return hi, lo


def _cparams(*sem):
    return pltpu.CompilerParams(dimension_semantics=sem, vmem_limit_bytes=VMEM_LIMIT_BYTES)


def _rmsnorm_kernel(x_ref, g_ref, o_ref):
    x = x_ref[...]
    ms = jnp.mean(x * x, axis=-1, keepdims=True)
    o_ref[...] = (x * lax.rsqrt(ms + EPS) * g_ref[...]).astype(o_ref.dtype)


def _rmsnorm(x, g, *, tm):
    m, d = x.shape
    return pl.pallas_call(
        _rmsnorm_kernel,
        grid=(m // tm,),
        in_specs=[pl.BlockSpec((tm, d), lambda i: (i, 0)), pl.BlockSpec((1, d), lambda i: (0, 0))],
        out_specs=pl.BlockSpec((tm, d), lambda i: (i, 0)),
        out_shape=jax.ShapeDtypeStruct((m, d), BF16),
        compiler_params=_cparams("parallel"),
        name="rmsnorm",
    )(x, g.reshape(1, d))


def _mm_body(a_refs, w_refs, extra, o_ref, epi, group):
    if epi == "swiglu":
        a = a_refs[0][...]
        gate = _dot(a, w_refs[0][...])
        up = _dot(a, w_refs[1][...])
        o_ref[...] = (gate * _sigmoid(gate) * up).astype(o_ref.dtype)
        return
    if epi == "merge":
        out = None
        for a_ref, w_ref, g_ref in zip(a_refs, w_refs, extra):
            y = g_ref[...].astype(F32) * _dot(a_ref[...], w_ref[...])
            out = y if out is None else out + y
        o_ref[...] = out.astype(o_ref.dtype)
        return
    acc = _dot(a_refs[0][...], w_refs[0][...])
    if epi == "cast":
        o_ref[...] = acc.astype(o_ref.dtype)
    elif epi == "sigmoid":
        o_ref[...] = _sigmoid(acc).astype(o_ref.dtype)
    elif epi == "resid":
        o_ref[...] = (extra[0][...] + acc).astype(o_ref.dtype)
    elif epi == "gnorm":
        gain = extra[0][...]
        for c0 in range(0, acc.shape[1], group):
            y = acc[:, c0:c0 + group]
            ms = jnp.mean(y * y, axis=-1, keepdims=True)
            o_ref[:, c0:c0 + group] = (y * lax.rsqrt(ms + EPS) * gain[:, c0:c0 + group]).astype(o_ref.dtype)
    else:
        raise ValueError(epi)


def _mm_wcast_kernel(*refs, n_a, n_w, epi, group, shift):
    a_refs = refs[:n_a]
    w_refs = refs[n_a:n_a + n_w]
    n_next = 1 if shift else 0
    next_refs = refs[n_a + n_w:n_a + n_w + n_next]
    extra = refs[n_a + n_w + n_next:-1 - n_w]
    o_ref = refs[-1 - n_w]
    wbf_refs = refs[-n_w:]

    @pl.when(pl.program_id(1) == 0)
    def _():
        if shift:
            tn = w_refs[0].shape[1]
            wide = jnp.concatenate([w_refs[0][...].astype(BF16), next_refs[0][...].astype(BF16)], axis=1)
            wbf_refs[0][...] = wide[:, shift:shift + tn]
        else:
            for w_ref, wbf_ref in zip(w_refs, wbf_refs):
                wbf_ref[...] = w_ref[...].astype(BF16)

    _mm_body(a_refs, wbf_refs, extra, o_ref, epi, group)


def _matmul_wcast(a_list, w_list, w_maps, extras, extra_specs, *, n, tm, tn, out_dtype, epi, name, group=0,
                  shift=0, next_map=None):
    m = a_list[0].shape[0]
    in_specs = [pl.BlockSpec((tm, a.shape[1]), lambda j, i: (i, 0)) for a in a_list]
    in_specs += [pl.BlockSpec((None, w.shape[1], tn), wm) for w, wm in zip(w_list, w_maps)]
    operands = list(a_list) + list(w_list)
    if shift:
        assert len(w_list) == 1 and 0 < shift < LANES
        in_specs.append(pl.BlockSpec((None, w_list[0].shape[1], LANES), next_map))
        operands.append(w_list[0])
    in_specs += list(extra_specs)
    kern = functools.partial(_mm_wcast_kernel, n_a=len(a_list), n_w=len(w_list), epi=epi, group=group, shift=shift)
    return pl.pallas_call(
        kern,
        grid=(n // tn, m // tm),
        in_specs=in_specs,
        out_specs=pl.BlockSpec((tm, tn), lambda j, i: (i, j)),
        out_shape=jax.ShapeDtypeStruct((m, n), out_dtype),
        scratch_shapes=[pltpu.VMEM((w.shape[1], tn), BF16) for w in w_list],
        compiler_params=_cparams("parallel", "arbitrary"),
        name=name,
    )(*operands, *extras)


def _gla_kernel(q_ref, k_ref, v_ref, r_ref, a1_ref, wa2_ref, ba_ref, gn_ref, o_ref, s_ref, *, bt):
    c_len = GLA_CHUNK

    @pl.when(pl.program_id(0) == 0)
    def _():
        s_ref[...] = jnp.zeros_like(s_ref)

    row = lax.broadcasted_iota(jnp.int32, (bt, bt), 0)
    col = lax.broadcasted_iota(jnp.int32, (bt, bt), 1)
    shift = c_len.bit_length() - 1
    same_chunk = (row >> shift) == (col >> shift)
    tri = jnp.logical_and(col <= row, same_chunk).astype(BF16)
    crow = lax.broadcasted_iota(jnp.int32, (c_len, c_len), 0)
    ccol = lax.broadcasted_iota(jnp.int32, (c_len, c_len), 1)
    causal = ccol <= crow

    pre = _dot(a1_ref[...], wa2_ref[...]) + ba_ref[...]
    log_a = _neg_softplus(-pre) * (1.0 / GLA_TAU)
    la_hi, la_lo = _split_bf16(log_a)
    b = _dot(tri, la_hi) + _dot(tri, la_lo)

    gn = gn_ref[...]
    scale = GLA_DK ** -0.5
    for c in range(bt // c_len):
        rs = slice(c * c_len, (c + 1) * c_len)
        for h in range(GLA_HEADS):
            ks = slice(h * GLA_DK, (h + 1) * GLA_DK)
            vs = slice(h * GLA_DV, (h + 1) * GLA_DV)
            bh = b[rs, ks]
            bt_ = bh.T
            b_last = bt_[:, c_len - 1:c_len]
            q = q_ref[rs, ks] * scale
            k_t = k_ref[rs, ks].T
            v = v_ref[rs, vs].astype(BF16)
            q_e = (q * jnp.exp(bh)).astype(BF16)
            k_e = (k_t * jnp.exp(-bt_)).astype(BF16)
            k_d = (k_t * jnp.exp(b_last - bt_)).astype(BF16)
            s = jnp.where(causal, _dot(q_e, k_e), 0.0).astype(BF16)
            state = s_ref[h]
            o = _dot(s, v) + _dot(q_e, state.astype(BF16))
            s_ref[h] = state * jnp.exp(b_last) + _dot(k_d, v)
            ms = jnp.mean(o * o, axis=-1, keepdims=True)
            o_n = o * lax.rsqrt(ms + EPS) * gn
            r = r_ref[rs, vs]
            o_ref[rs, vs] = (o_n * (r * _sigmoid(r))).astype(o_ref.dtype)


def _gla(p_gla, a1, wa2p, ba, gn, *, bt):
    t = p_gla.shape[0]
    hk = GLA_HEADS * GLA_DK
    hv = GLA_HEADS * GLA_DV
    return pl.pallas_call(
        functools.partial(_gla_kernel, bt=bt),
        grid=(t // bt,),
        in_specs=[
            pl.BlockSpec((bt, hk), lambda i: (i, 0)),
            pl.BlockSpec((bt, hk), lambda i: (i, 1)),
            pl.BlockSpec((bt, hv), lambda i: (i, 1)),
            pl.BlockSpec((bt, hv), lambda i: (i, 2)),
            pl.BlockSpec((bt, LANES), lambda i: (i, 0)),
            pl.BlockSpec((LANES, hk), lambda i: (0, 0)),
            pl.BlockSpec((1, hk), lambda i: (0, 0)),
            pl.BlockSpec((1, GLA_DV), lambda i: (0, 0)),
        ],
        out_specs=pl.BlockSpec((bt, hv), lambda i: (i, 0)),
        out_shape=jax.ShapeDtypeStruct((t, hv), BF16),
        scratch_shapes=[pltpu.VMEM((GLA_HEADS, GLA_DK, GLA_DV), F32)],
        compiler_params=_cparams("arbitrary"),
        name="gla",
    )(p_gla, p_gla, p_gla, p_gla, a1, wa2p, ba.reshape(1, hk), gn.reshape(1, GLA_DV))


def _sb_kernel(q_ref, k_ref, v_ref, o_ref, *, tile, heads):
    i = pl.program_id(1)
    row = lax.broadcasted_iota(jnp.int32, (tile, tile), 0)
    col = lax.broadcasted_iota(jnp.int32, (tile, tile), 1)
    from_here = (row >= col).astype(BF16)

    def step_head(hd, j, c, acc, diag):
        cs = slice(hd * SB_DH, (hd + 1) * SB_DH)
        start = pl.multiple_of(j * tile, tile)
        q = q_ref[:, cs]
        k = k_ref[pl.ds(start, tile), cs]
        v = v_ref[pl.ds(start, tile), cs]
        z = lax.dot_general(q, k, _NT, preferred_element_type=F32)
        l = _softplus2(z)
        if diag:
            causal = col < row
            l = jnp.where(causal, l, 0.0)
        e = z - _dot(l.astype(BF16), from_here)
        if c is not None:
            e = e - c
        a = jnp.exp2(e)
        if diag:
            a = jnp.where(causal, a, 0.0)
        pv = _dot(a.astype(BF16), v)
        tot = jnp.sum(l, axis=-1, keepdims=True)
        if c is None:
            return tot, pv
        return c + tot, acc + pv

    def step(j, state, diag):
        if state is None:
            state = [(None, None)] * heads
        return tuple(step_head(hd, j, c, acc, diag) for hd, (c, acc) in enumerate(state))

    def only_diag():
        return step(i, None, True)

    def diag_and_previous():
        return step(i - 1, step(i, None, True), False)

    state = lax.cond(i == 0, only_diag, diag_and_previous)

    def cond(carry):
        j, state = carry
        c_min = functools.reduce(jnp.minimum, [jnp.min(c) for c, _ in state])
        return jnp.logical_and(j >= 0, c_min <= F32_EXP2_UNDERFLOW)

    def body(carry):
        j, state = carry
        return j - 1, step(j, state, False)

    _, state = lax.while_loop(cond, body, (i - 2, state))
    for hd, (_, acc) in enumerate(state):
        o_ref[:, hd * SB_DH:(hd + 1) * SB_DH] = acc.astype(o_ref.dtype)


def _sb_attention(qk, v, *, tile, heads):
    t = v.shape[0]
    groups = SB_HEADS // heads
    w = heads * SB_DH
    return pl.pallas_call(
        functools.partial(_sb_kernel, tile=tile, heads=heads),
        grid=(groups, t // tile),
        in_specs=[
            pl.BlockSpec((tile, w), lambda g, i: (i, g)),
            pl.BlockSpec((t, w), lambda g, i: (0, groups + g)),
            pl.BlockSpec((t, w), lambda g, i: (0, g)),
        ],
        out_specs=pl.BlockSpec((tile, w), lambda g, i: (i, g)),
        out_shape=jax.ShapeDtypeStruct((t, SB_HEADS * SB_DH), BF16),
        compiler_params=_cparams("parallel", "parallel"),
        name="sb_attention",
    )(qk, qk, v)


def _mem_kernel(q_ref, k_ref, v_ref, o_ref):
    scale = MEM_DH ** -0.5
    for h in range(MEM_HEADS):
        cs = slice(h * MEM_DH, (h + 1) * MEM_DH)
        s = lax.dot_general(q_ref[:, cs], k_ref[:, cs], _NT, preferred_element_type=F32) * scale
        e = jnp.exp(s - jnp.max(s, axis=-1, keepdims=True))
        p = e / jnp.sum(e, axis=-1, keepdims=True)
        o_ref[:, cs] = _dot(p.astype(BF16), v_ref[:, cs]).astype(o_ref.dtype)


def _mem_attention(q, mk, mv, *, tq):
    t, w = q.shape
    m = mk.shape[0]
    return pl.pallas_call(
        _mem_kernel,
        grid=(t // tq,),
        in_specs=[
            pl.BlockSpec((tq, w), lambda i: (i, 0)),
            pl.BlockSpec((m, w), lambda i: (0, 0)),
            pl.BlockSpec((m, w), lambda i: (0, 0)),
        ],
        out_specs=pl.BlockSpec((tq, w), lambda i: (i, 0)),
        out_shape=jax.ShapeDtypeStruct((t, w), BF16),
        compiler_params=_cparams("parallel"),
        name="mem_attention",
    )(q, mk, mv)


PROJ_TN = 1024


def _w_in_segments(d):
    hk = GLA_HEADS * GLA_DK
    hv = GLA_HEADS * GLA_DV
    mix = SB_HEADS * SB_DH
    sizes = {"gla": 2 * hk + 2 * hv, "a1": GLA_RANK, "sb_qk": 2 * mix, "sb_v": mix, "mem_q": MEM_HEADS * MEM_DH,
             "gates": N_BRANCH * d}
    offs, src = {}, 0
    for name, size in sizes.items():
        offs[name] = src
        src += size
    return offs


def _layer(x, mem, p, layer):
    t, d = x.shape
    m = mem.shape[0]
    hk = GLA_HEADS * GLA_DK
    hv = GLA_HEADS * GLA_DV
    mix = SB_HEADS * SB_DH
    mw = MEM_HEADS * MEM_DH
    d_ff = p["w_down"].shape[1]
    seg = _w_in_segments(d)

    def wcol(off=0):
        return lambda j, i: (layer, 0, j + off)

    def proj(a, name, n, out_dtype, epi, *, tn=PROJ_TN, group=0, extras=(), extra_specs=()):
        shift = seg[name] % LANES
        base = seg[name] - shift
        assert base % tn == 0
        next_map = lambda j, i: (layer, 0, (base + (j + 1) * tn) // LANES)
        return _matmul_wcast([a], [p["w_in"]], [wcol(base // tn)], list(extras), list(extra_specs), n=n, tm=1024,
                             tn=tn, out_dtype=out_dtype, epi=epi, group=group, name="proj_" + name, shift=shift,
                             next_map=next_map)

    def tile_spec(tm, tn, off=0):
        return pl.BlockSpec((tm, tn), lambda j, i: (i, j + off))

    gain_spec = pl.BlockSpec((1, PROJ_TN), lambda j, i: (0, j))

    h = _rmsnorm(x, p["attn_norm"][layer], tm=512)

    p_gla = proj(h, "gla", 2 * hk + 2 * hv, F32, "cast")
    a1 = proj(h, "a1", LANES, BF16, "cast", tn=LANES)
    wa2p = jnp.pad(p["gla_w_a2"][layer], ((0, LANES - GLA_RANK), (0, 0))).astype(BF16)
    o_gla = _gla(p_gla, a1, wa2p, p["gla_b_a"][layer], p["gla_out_norm"][layer], bt=256)

    q_gain = p["sb_q_norm"][layer] * (SB_DH ** -0.5 * LOG2_E)
    qk_gain = jnp.concatenate([jnp.tile(q_gain, SB_HEADS), jnp.tile(p["sb_k_norm"][layer], SB_HEADS)]).reshape(1, -1)
    qk = proj(h, "sb_qk", 2 * mix, BF16, "gnorm", group=SB_DH, extras=[qk_gain], extra_specs=[gain_spec])
    sv = proj(h, "sb_v", mix, BF16, "cast")
    o_sb = _sb_attention(qk, sv, tile=256, heads=2)

    hm = _rmsnorm(mem, p["mem_norm"][layer], tm=m)
    mk_gain = jnp.tile(p["mem_k_norm"][layer], MEM_HEADS).reshape(1, -1)
    mq_gain = jnp.tile(p["mem_q_norm"][layer], MEM_HEADS).reshape(1, -1)
    m_k = _matmul_wcast([hm], [p["w_mem_kv"]], [wcol()], [mk_gain], [gain_spec], n=mw, tm=m, tn=mw,
                        out_dtype=BF16, epi="gnorm", group=MEM_DH, name="proj_mem_k")
    m_v = _matmul_wcast([hm], [p["w_mem_kv"]], [wcol(1)], [], [], n=mw, tm=m, tn=mw, out_dtype=BF16, epi="cast",
                        name="proj_mem_v")
    q_m = proj(h, "mem_q", mw, BF16, "gnorm", group=MEM_DH, extras=[mq_gain], extra_specs=[gain_spec])
    o_mem = _mem_attention(q_m, m_k, m_v, tq=512)

    gates = proj(h, "gates", N_BRANCH * d, BF16, "sigmoid")
    tm, tn = 512, 1024
    nj = d // tn
    merged = _matmul_wcast(
        [o_gla, o_sb, o_mem], [p["w_br_gla"], p["w_br_sb"], p["w_br_mem"]], [wcol(), wcol(), wcol()],
        [gates, gates, gates], [tile_spec(tm, tn, b * nj) for b in range(N_BRANCH)],
        n=d, tm=tm, tn=tn, out_dtype=BF16, epi="merge", name="branch_merge")
    tm, tn = 1024, 1024
    x = _matmul_wcast([merged], [p["w_o"]], [wcol()], [x], [tile_spec(tm, tn)],
                      n=d, tm=tm, tn=tn, out_dtype=F32, epi="resid", name="out_proj")

    h2 = _rmsnorm(x, p["ffn_norm"][layer], tm=512)
    tm, tn = 1024, 512
    act = _matmul_wcast([h2], [p["w_gate_up"], p["w_gate_up"]], [wcol(), wcol(d_ff // tn)], [], [],
                        n=d_ff, tm=tm, tn=tn, out_dtype=BF16, epi="swiglu", name="ffn_gate_up")
    tm, tn = 512, 512
    x = _matmul_wcast([act], [p["w_down"]], [wcol()], [x], [tile_spec(tm, tn)],
                      n=d, tm=tm, tn=tn, out_dtype=F32, epi="resid", name="ffn_down")
    return x


_PARAM_NAMES = ("attn_norm", "w_in", "gla_w_a2", "gla_b_a", "gla_out_norm", "w_br_gla", "sb_q_norm", "sb_k_norm",
                "w_br_sb", "mem_norm", "w_mem_kv", "mem_q_norm", "mem_k_norm", "w_br_mem", "w_o", "ffn_norm",
                "w_gate_up", "w_down")


def kernel(x, mem, attn_norm, w_in, gla_w_a2, gla_b_a, gla_out_norm, w_br_gla, sb_q_norm, sb_k_norm, w_br_sb,
           mem_norm, w_mem_kv, mem_q_norm, mem_k_norm, w_br_mem, w_o, ffn_norm, w_gate_up, w_down):
    params = dict(zip(_PARAM_NAMES, (attn_norm, w_in, gla_w_a2, gla_b_a, gla_out_norm, w_br_gla, sb_q_norm,
                                     sb_k_norm, w_br_sb, mem_norm, w_mem_kv, mem_q_norm, mem_k_norm, w_br_mem,
                                     w_o, ffn_norm, w_gate_up, w_down)))
    b, t, d = x.shape
    assert b == 1, "kernels are written for a single sequence"
    xs = x.reshape(t, d)
    ms = mem.reshape(mem.shape[1], d)
    for layer in range(w_in.shape[0]):
        xs = _layer(xs, ms, params, layer)
    return xs.reshape(b, t, d)
```

```python
import functools
import math

import jax
import jax.numpy as jnp
from jax import lax
from jax.experimental import pallas as pl
from jax.experimental.pallas import tpu as pltpu

F32 = jnp.float32
BF16 = jnp.bfloat16
EPS = 1e-6

GLA_HEADS = 4
GLA_DK = 128
GLA_DV = 256
GLA_RANK = 16
GLA_TAU = 16.0
GLA_CHUNK = 64
SB_HEADS = 8
SB_DH = 128
MEM_HEADS = 4
MEM_DH = 256
N_BRANCH = 3

LANES = 128
VMEM_LIMIT_BYTES = 56 * 2**20
F32_EXP2_UNDERFLOW = 150.0
LOG2_E = math.log2(math.e)

_NT = (((1,), (1,)), ((), ()))
_TN = (((0,), (0,)), ((), ()))


def _dot(a, b):
    return jnp.dot(a, b, preferred_element_type=F32)


def _dot_nt(a, b):
    return lax.dot_general(a, b, _NT, preferred_element_type=F32)


def _softplus2(z2):
    return jnp.where(z2 > 64.0, z2, jnp.log2(1.0 + jnp.exp2(z2)))


def _sigmoid(z):
    return 1.0 / (1.0 + jnp.exp(-z))


def _split_bf16(v):
    hi = v.astype(BF16)
    lo = (v - hi.astype(F32)).astype(BF16)
    return hi, lo


def _cparams(*sem):
    return pltpu.CompilerParams(dimension_semantics=sem, vmem_limit_bytes=VMEM_LIMIT_BYTES)


def _rmsnorm_kernel(x_ref, g_ref, o_ref):
    x = x_ref[...]
    ms = jnp.mean(x * x, axis=-1, keepdims=True)
    o_ref[...] = (x * lax.rsqrt(ms + EPS) * g_ref[...]).astype(o_ref.dtype)


def _rmsnorm(x, g, *, tm):
    m, d = x.shape
    return pl.pallas_call(
        _rmsnorm_kernel,
        grid=(m // tm,),
        in_specs=[pl.BlockSpec((tm, d), lambda i: (i, 0)), pl.BlockSpec((1, d), lambda i: (0, 0))],
        out_specs=pl.BlockSpec((tm, d), lambda i: (i, 0)),
        out_shape=jax.ShapeDtypeStruct((m, d), BF16),
        compiler_params=_cparams("parallel"),
        name="rmsnorm",
    )(x, g.reshape(1, d))


def _mm_body(a_refs, w_refs, extra, o_ref, epi, group, w_rows):
    mm = _dot_nt if w_rows else _dot
    if epi == "swiglu":
        a = a_refs[0][...]
        gate = mm(a, w_refs[0][...])
        up = mm(a, w_refs[1][...])
        o_ref[...] = (gate * _sigmoid(gate) * up).astype(o_ref.dtype)
        return
    if epi == "merge":
        out = None
        for a_ref, w_ref, g_ref in zip(a_refs, w_refs, extra):
            y = g_ref[...].astype(F32) * mm(a_ref[...], w_ref[...])
            out = y if out is None else out + y
        o_ref[...] = out.astype(o_ref.dtype)
        return
    acc = mm(a_refs[0][...], w_refs[0][...])
    if epi == "cast":
        o_ref[...] = acc.astype(o_ref.dtype)
    elif epi == "sigmoid":
        o_ref[...] = _sigmoid(acc).astype(o_ref.dtype)
    elif epi == "resid":
        o_ref[...] = (extra[0][...] + acc).astype(o_ref.dtype)
    elif epi == "gnorm":
        gain = extra[0][...]
        for c0 in range(0, acc.shape[1], group):
            y = acc[:, c0:c0 + group]
            ms = jnp.mean(y * y, axis=-1, keepdims=True)
            o_ref[:, c0:c0 + group] = (y * lax.rsqrt(ms + EPS) * gain[:, c0:c0 + group]).astype(o_ref.dtype)
    else:
        raise ValueError(epi)


def _mm_wcast_kernel(*refs, n_a, n_w, epi, group, w_rows, shift):
    a_refs = refs[:n_a]
    w_refs = refs[n_a:n_a + n_w]
    n_next = 1 if shift else 0
    next_refs = refs[n_a + n_w:n_a + n_w + n_next]
    extra = refs[n_a + n_w + n_next:-1 - n_w]
    o_ref = refs[-1 - n_w]
    wbf_refs = refs[-n_w:]

    @pl.when(pl.program_id(1) == 0)
    def _():
        if shift:
            tn = w_refs[0].shape[0]
            wbf_refs[0][0:tn - shift, :] = w_refs[0][shift:tn, :].astype(BF16)
            wbf_refs[0][tn - shift:tn, :] = next_refs[0][...].astype(BF16)
        else:
            for w_ref, wbf_ref in zip(w_refs, wbf_refs):
                wbf_ref[...] = w_ref[...].astype(BF16)

    _mm_body(a_refs, wbf_refs, extra, o_ref, epi, group, w_rows)


def _matmul_wcast(a_list, w_list, w_maps, extras, extra_specs, *, n, tm, tn, out_dtype, epi, name, group=0,
                  w_rows=False, shift=0, next_map=None):
    m = a_list[0].shape[0]
    in_specs = [pl.BlockSpec((tm, a.shape[1]), lambda j, i: (i, 0)) for a in a_list]
    w_blocks = [(tn, w.shape[2]) if w_rows else (w.shape[1], tn) for w in w_list]
    in_specs += [pl.BlockSpec((None,) + blk, wm) for blk, wm in zip(w_blocks, w_maps)]
    operands = list(a_list) + list(w_list)
    if shift:
        assert w_rows and len(w_list) == 1 and shift % 16 == 0 and tn % shift == 0
        in_specs.append(pl.BlockSpec((None, shift, w_list[0].shape[2]), next_map))
        operands.append(w_list[0])
    in_specs += list(extra_specs)
    kern = functools.partial(_mm_wcast_kernel, n_a=len(a_list), n_w=len(w_list), epi=epi, group=group,
                             w_rows=w_rows, shift=shift)
    return pl.pallas_call(
        kern,
        grid=(n // tn, m // tm),
        in_specs=in_specs,
        out_specs=pl.BlockSpec((tm, tn), lambda j, i: (i, j)),
        out_shape=jax.ShapeDtypeStruct((m, n), out_dtype),
        scratch_shapes=[pltpu.VMEM(blk, BF16) for blk in w_blocks],
        compiler_params=_cparams("parallel", "arbitrary"),
        name=name,
    )(*operands, *extras)


def _gla_kernel(q_ref, k_ref, v_ref, r_ref, h_ref, wa1_ref, wa2_ref, ba_ref, gn_ref, o_ref, s_ref, wa1_bf_ref, *, bt):
    c_len = GLA_CHUNK

    @pl.when(pl.program_id(0) == 0)
    def _():
        s_ref[...] = jnp.zeros_like(s_ref)
        wa1_bf_ref[...] = wa1_ref[...].astype(BF16)

    row = lax.broadcasted_iota(jnp.int32, (bt, bt), 0)
    col = lax.broadcasted_iota(jnp.int32, (bt, bt), 1)
    shift = c_len.bit_length() - 1
    same_chunk = (row >> shift) == (col >> shift)
    tri = jnp.logical_and(col <= row, same_chunk).astype(BF16)
    crow = lax.broadcasted_iota(jnp.int32, (c_len, c_len), 0)
    ccol = lax.broadcasted_iota(jnp.int32, (c_len, c_len), 1)
    causal = ccol <= crow

    a1 = _dot_nt(h_ref[...], wa1_bf_ref[...]).astype(BF16)
    pre = _dot(a1, wa2_ref[...]) + ba_ref[...]
    log_a = _softplus2(pre * -LOG2_E) * (-1.0 / GLA_TAU)
    la_hi, la_lo = _split_bf16(log_a)
    b = _dot(tri, la_hi) + _dot(tri, la_lo)

    gn = gn_ref[...]
    scale = GLA_DK ** -0.5
    for c in range(bt // c_len):
        rs = slice(c * c_len, (c + 1) * c_len)
        for h in range(GLA_HEADS):
            ks = slice(h * GLA_DK, (h + 1) * GLA_DK)
            vs = slice(h * GLA_DV, (h + 1) * GLA_DV)
            bh = b[rs, ks]
            b_last = bh[c_len - 1:c_len, :]
            k = k_ref[rs, ks]
            v = v_ref[rs, vs].astype(BF16)
            q_e = (q_ref[rs, ks] * (scale * jnp.exp2(bh))).astype(BF16)
            k_e = (k * jnp.exp2(-bh)).astype(BF16)
            k_d = (k * jnp.exp2(b_last - bh)).astype(BF16)
            s = jnp.where(causal, _dot_nt(q_e, k_e), 0.0).astype(BF16)
            state_t = s_ref[h]
            o = _dot(s, v) + _dot_nt(q_e, state_t.astype(BF16))
            s_ref[h] = state_t * jnp.exp2(b_last) + lax.dot_general(v, k_d, _TN, preferred_element_type=F32)
            ms = jnp.mean(o * o, axis=-1, keepdims=True)
            o_n = o * lax.rsqrt(ms + EPS) * gn
            r = r_ref[rs, vs]
            o_ref[rs, vs] = (o_n * (r * _sigmoid(r))).astype(o_ref.dtype)


def _gla(p_gla, h, w_in_t, a1_block, wa2p, ba, gn, *, bt):
    t, d = h.shape
    hk = GLA_HEADS * GLA_DK
    hv = GLA_HEADS * GLA_DV
    return pl.pallas_call(
        functools.partial(_gla_kernel, bt=bt),
        grid=(t // bt,),
        in_specs=[
            pl.BlockSpec((bt, hk), lambda i: (i, 0)),
            pl.BlockSpec((bt, hk), lambda i: (i, 1)),
            pl.BlockSpec((bt, hv), lambda i: (i, 1)),
            pl.BlockSpec((bt, hv), lambda i: (i, 2)),
            pl.BlockSpec((bt, d), lambda i: (i, 0)),
            pl.BlockSpec((None, LANES, d), lambda i: a1_block),
            pl.BlockSpec((LANES, hk), lambda i: (0, 0)),
            pl.BlockSpec((1, hk), lambda i: (0, 0)),
            pl.BlockSpec((1, GLA_DV), lambda i: (0, 0)),
        ],
        out_specs=pl.BlockSpec((bt, hv), lambda i: (i, 0)),
        out_shape=jax.ShapeDtypeStruct((t, hv), BF16),
        scratch_shapes=[pltpu.VMEM((GLA_HEADS, GLA_DV, GLA_DK), F32), pltpu.VMEM((LANES, d), BF16)],
        compiler_params=_cparams("arbitrary"),
        name="gla",
    )(p_gla, p_gla, p_gla, p_gla, h, w_in_t, wa2p, ba.reshape(1, hk), gn.reshape(1, GLA_DV))


def _sb_kernel(q_ref, k_ref, v_ref, o_ref, *, tile, heads):
    i = pl.program_id(1)
    row = lax.broadcasted_iota(jnp.int32, (tile, tile), 0)
    col = lax.broadcasted_iota(jnp.int32, (tile, tile), 1)
    from_here = (row >= col).astype(BF16)

    def step_head(hd, j, c, acc, diag):
        cs = slice(hd * SB_DH, (hd + 1) * SB_DH)
        start = pl.multiple_of(j * tile, tile)
        q = q_ref[:, cs]
        k = k_ref[pl.ds(start, tile), cs]
        v = v_ref[pl.ds(start, tile), cs]
        z = _dot_nt(q, k)
        l = _softplus2(z)
        if diag:
            causal = col < row
            l = jnp.where(causal, l, 0.0)
        e = z - _dot(l.astype(BF16), from_here)
        if c is not None:
            e = e - c
        a = jnp.exp2(e)
        if diag:
            a = jnp.where(causal, a, 0.0)
        pv = _dot(a.astype(BF16), v)
        tot = jnp.sum(l, axis=-1, keepdims=True)
        if c is None:
            return tot, pv
        return c + tot, acc + pv

    def step(j, state, diag):
        if state is None:
            state = [(None, None)] * heads
        return tuple(step_head(hd, j, c, acc, diag) for hd, (c, acc) in enumerate(state))

    def only_diag():
        return step(i, None, True)

    def diag_and_previous():
        return step(i - 1, step(i, None, True), False)

    state = lax.cond(i == 0, only_diag, diag_and_previous)

    def cond(carry):
        j, state = carry
        c_min = functools.reduce(jnp.minimum, [jnp.min(c) for c, _ in state])
        return jnp.logical_and(j >= 0, c_min <= F32_EXP2_UNDERFLOW)

    def body(carry):
        j, state = carry
        return j - 1, step(j, state, False)

    _, state = lax.while_loop(cond, body, (i - 2, state))
    for hd, (_, acc) in enumerate(state):
        o_ref[:, hd * SB_DH:(hd + 1) * SB_DH] = acc.astype(o_ref.dtype)


def _sb_attention(qk, v, *, tile, heads):
    t = v.shape[0]
    groups = SB_HEADS // heads
    w = heads * SB_DH
    return pl.pallas_call(
        functools.partial(_sb_kernel, tile=tile, heads=heads),
        grid=(groups, t // tile),
        in_specs=[
            pl.BlockSpec((tile, w), lambda g, i: (i, g)),
            pl.BlockSpec((t, w), lambda g, i: (0, groups + g)),
            pl.BlockSpec((t, w), lambda g, i: (0, g)),
        ],
        out_specs=pl.BlockSpec((tile, w), lambda g, i: (i, g)),
        out_shape=jax.ShapeDtypeStruct((t, SB_HEADS * SB_DH), BF16),
        compiler_params=_cparams("parallel", "parallel"),
        name="sb_attention",
    )(qk, qk, v)


def _mem_kernel(q_ref, k_ref, v_ref, o_ref):
    scale = MEM_DH ** -0.5
    for h in range(MEM_HEADS):
        cs = slice(h * MEM_DH, (h + 1) * MEM_DH)
        s = _dot_nt(q_ref[:, cs], k_ref[:, cs]) * scale
        e = jnp.exp(s - jnp.max(s, axis=-1, keepdims=True))
        p = e / jnp.sum(e, axis=-1, keepdims=True)
        o_ref[:, cs] = _dot(p.astype(BF16), v_ref[:, cs]).astype(o_ref.dtype)


def _mem_attention(q, mk, mv, *, tq):
    t, w = q.shape
    m = mk.shape[0]
    return pl.pallas_call(
        _mem_kernel,
        grid=(t // tq,),
        in_specs=[
            pl.BlockSpec((tq, w), lambda i: (i, 0)),
            pl.BlockSpec((m, w), lambda i: (0, 0)),
            pl.BlockSpec((m, w), lambda i: (0, 0)),
        ],
        out_specs=pl.BlockSpec((tq, w), lambda i: (i, 0)),
        out_shape=jax.ShapeDtypeStruct((t, w), BF16),
        compiler_params=_cparams("parallel"),
        name="mem_attention",
    )(q, mk, mv)


PROJ_TN = 1024


def _w_in_segments(d):
    hk = GLA_HEADS * GLA_DK
    hv = GLA_HEADS * GLA_DV
    mix = SB_HEADS * SB_DH
    sizes = {"gla": 2 * hk + 2 * hv, "a1": GLA_RANK, "sb_qk": 2 * mix, "sb_v": mix, "mem_q": MEM_HEADS * MEM_DH,
             "gates": N_BRANCH * d}
    offs, src = {}, 0
    for name, size in sizes.items():
        offs[name] = src
        src += size
    return offs


def _layer(x, mem, p, layer):
    t, d = x.shape
    m = mem.shape[0]
    hk = GLA_HEADS * GLA_DK
    hv = GLA_HEADS * GLA_DV
    mix = SB_HEADS * SB_DH
    mw = MEM_HEADS * MEM_DH
    d_ff = p["w_down"].shape[1]
    seg = _w_in_segments(d)

    def wcol(off=0):
        return lambda j, i: (layer, 0, j + off)

    def proj(a, name, n, out_dtype, epi, *, tn=PROJ_TN, group=0, extras=(), extra_specs=()):
        shift = seg[name] % tn
        base = seg[name] - shift
        rows = lambda j, i: (layer, base // tn + j, 0)
        next_map = (lambda j, i: (layer, (base + (j + 1) * tn) // shift, 0)) if shift else None
        return _matmul_wcast([a], [p["w_in_t"]], [rows], list(extras), list(extra_specs), n=n, tm=1024, tn=tn,
                             out_dtype=out_dtype, epi=epi, group=group, name="proj_" + name, w_rows=True,
                             shift=shift, next_map=next_map)

    def tile_spec(tm, tn, off=0):
        return pl.BlockSpec((tm, tn), lambda j, i: (i, j + off))

    gain_spec = pl.BlockSpec((1, PROJ_TN), lambda j, i: (0, j))

    h = _rmsnorm(x, p["attn_norm"][layer], tm=512)

    p_gla = proj(h, "gla", 2 * hk + 2 * hv, F32, "cast")
    assert seg["a1"] % LANES == 0
    wa2p = jnp.pad(p["gla_w_a2"][layer], ((0, LANES - GLA_RANK), (0, 0))).astype(BF16)
    o_gla = _gla(p_gla, h, p["w_in_t"], (layer, seg["a1"] // LANES, 0), wa2p, p["gla_b_a"][layer],
                 p["gla_out_norm"][layer], bt=256)

    q_gain = p["sb_q_norm"][layer] * (SB_DH ** -0.5 * LOG2_E)
    qk_gain = jnp.concatenate([jnp.tile(q_gain, SB_HEADS), jnp.tile(p["sb_k_norm"][layer], SB_HEADS)]).reshape(1, -1)
    qk = proj(h, "sb_qk", 2 * mix, BF16, "gnorm", group=SB_DH, extras=[qk_gain], extra_specs=[gain_spec])
    sv = proj(h, "sb_v", mix, BF16, "cast")
    o_sb = _sb_attention(qk, sv, tile=256, heads=2)

    hm = _rmsnorm(mem, p["mem_norm"][layer], tm=m)
    mk_gain = jnp.tile(p["mem_k_norm"][layer], MEM_HEADS).reshape(1, -1)
    mq_gain = jnp.tile(p["mem_q_norm"][layer], MEM_HEADS).reshape(1, -1)
    m_k = _matmul_wcast([hm], [p["w_mem_kv"]], [wcol()], [mk_gain], [gain_spec], n=mw, tm=m, tn=mw,
                        out_dtype=BF16, epi="gnorm", group=MEM_DH, name="proj_mem_k")
    m_v = _matmul_wcast([hm], [p["w_mem_kv"]], [wcol(1)], [], [], n=mw, tm=m, tn=mw, out_dtype=BF16, epi="cast",
                        name="proj_mem_v")
    q_m = proj(h, "mem_q", mw, BF16, "gnorm", group=MEM_DH, extras=[mq_gain], extra_specs=[gain_spec])
    o_mem = _mem_attention(q_m, m_k, m_v, tq=512)

    gates = proj(h, "gates", N_BRANCH * d, BF16, "sigmoid")
    tm, tn = 512, 1024
    nj = d // tn
    merged = _matmul_wcast(
        [o_gla, o_sb, o_mem], [p["w_br_gla"], p["w_br_sb"], p["w_br_mem"]], [wcol(), wcol(), wcol()],
        [gates, gates, gates], [tile_spec(tm, tn, b * nj) for b in range(N_BRANCH)],
        n=d, tm=tm, tn=tn, out_dtype=BF16, epi="merge", name="branch_merge")
    tm, tn = 1024, 1024
    x = _matmul_wcast([merged], [p["w_o"]], [wcol()], [x], [tile_spec(tm, tn)],
                      n=d, tm=tm, tn=tn, out_dtype=F32, epi="resid", name="out_proj")

    h2 = _rmsnorm(x, p["ffn_norm"][layer], tm=512)
    tm, tn = 1024, 512
    act = _matmul_wcast([h2], [p["w_gate_up"], p["w_gate_up"]], [wcol(), wcol(d_ff // tn)], [], [],
                        n=d_ff, tm=tm, tn=tn, out_dtype=BF16, epi="swiglu", name="ffn_gate_up")
    tm, tn = 512, 512
    x = _matmul_wcast([act], [p["w_down"]], [wcol()], [x], [tile_spec(tm, tn)],
                      n=d, tm=tm, tn=tn, out_dtype=F32, epi="resid", name="ffn_down")
    return x


_PARAM_NAMES = ("attn_norm", "w_in", "gla_w_a2", "gla_b_a", "gla_out_norm", "w_br_gla", "sb_q_norm", "sb_k_norm",
                "w_br_sb", "mem_norm", "w_mem_kv", "mem_q_norm", "mem_k_norm", "w_br_mem", "w_o", "ffn_norm",
                "w_gate_up", "w_down")


def kernel(x, mem, attn_norm, w_in, gla_w_a2, gla_b_a, gla_out_norm, w_br_gla, sb_q_norm, sb_k_norm, w_br_sb,
           mem_norm, w_mem_kv, mem_q_norm, mem_k_norm, w_br_mem, w_o, ffn_norm, w_gate_up, w_down):
    params = dict(zip(_PARAM_NAMES, (attn_norm, w_in, gla_w_a2, gla_b_a, gla_out_norm, w_br_gla, sb_q_norm,
                                     sb_k_norm, w_br_sb, mem_norm, w_mem_kv, mem_q_norm, mem_k_norm, w_br_mem,
                                     w_o, ffn_norm, w_gate_up, w_down)))
    b, t, d = x.shape
    assert b == 1, "kernels are written for a single sequence"
    xs = x.reshape(t, d)
    ms = mem.reshape(mem.shape[1], d)
    params["w_in_t"] = jnp.swapaxes(w_in, 1, 2)
    for layer in range(w_in.shape[0]):
        xs = _layer(xs, ms, params, layer)
    return xs.reshape(b, t, d)
```

```python
import functools
import math

import jax
import jax.numpy as jnp
from jax import lax
from jax.experimental import pallas as pl
from jax.experimental.pallas import tpu as pltpu

F32 = jnp.float32
BF16 = jnp.bfloat16
EPS = 1e-6

GLA_HEADS = 4
GLA_DK = 128
GLA_DV = 256
GLA_RANK = 16
GLA_TAU = 16.0
GLA_CHUNK = 64
SB_HEADS = 8
SB_DH = 128
MEM_HEADS = 4
MEM_DH = 256
N_BRANCH = 3

LANES = 128
VMEM_LIMIT_BYTES = 56 * 2**20
F32_EXP2_UNDERFLOW = 150.0
LOG2_E = math.log2(math.e)

_NT = (((1,), (1,)), ((), ()))
_TN = (((0,), (0,)), ((), ()))


def _dot(a, b):
    return jnp.dot(a, b, preferred_element_type=F32)


def _dot_nt(a, b):
    return lax.dot_general(a, b, _NT, preferred_element_type=F32)


def _softplus2(z2):
    return jnp.where(z2 > 64.0, z2, jnp.log2(1.0 + jnp.exp2(z2)))


def _sigmoid(z):
    return 0.5 * jnp.tanh(0.5 * z) + 0.5


def _split_bf16(v):
    hi = v.astype(BF16)
    lo = (v - hi.astype(F32)).astype(BF16)
    return hi, lo


def _cparams(*sem):
    return pltpu.CompilerParams(dimension_semantics=sem, vmem_limit_bytes=VMEM_LIMIT_BYTES)


def _rmsnorm_kernel(x_ref, g_ref, o_ref):
    x = x_ref[...]
    ms = jnp.mean(x * x, axis=-1, keepdims=True)
    o_ref[...] = (x * lax.rsqrt(ms + EPS) * g_ref[...]).astype(o_ref.dtype)


def _rmsnorm(x, g, *, tm):
    m, d = x.shape
    return pl.pallas_call(
        _rmsnorm_kernel,
        grid=(m // tm,),
        in_specs=[pl.BlockSpec((tm, d), lambda i: (i, 0)), pl.BlockSpec((1, d), lambda i: (0, 0))],
        out_specs=pl.BlockSpec((tm, d), lambda i: (i, 0)),
        out_shape=jax.ShapeDtypeStruct((m, d), BF16),
        compiler_params=_cparams("parallel"),
        name="rmsnorm",
    )(x, g.reshape(1, d))


def _cast_kernel(w_ref, o_ref):
    o_ref[...] = w_ref[...].astype(o_ref.dtype)


def _cast_bf16(w, *, tk):
    n_l, k, n = w.shape
    return pl.pallas_call(
        _cast_kernel,
        grid=(n_l, k // tk),
        in_specs=[pl.BlockSpec((None, tk, n), lambda l, i: (l, i, 0))],
        out_specs=pl.BlockSpec((None, tk, n), lambda l, i: (l, i, 0)),
        out_shape=jax.ShapeDtypeStruct(w.shape, BF16),
        compiler_params=_cparams("parallel", "parallel"),
        name="cast_bf16",
    )(w)


def _mm_body(a_refs, w_refs, extra, o_ref, epi, group, w_rows):
    mm = _dot_nt if w_rows else _dot
    if epi == "swiglu":
        a = a_refs[0][...]
        gate = mm(a, w_refs[0][...])
        up = mm(a, w_refs[1][...])
        o_ref[...] = (gate * _sigmoid(gate) * up).astype(o_ref.dtype)
        return
    if epi == "merge":
        out = None
        for a_ref, w_ref, g_ref in zip(a_refs, w_refs, extra):
            y = g_ref[...].astype(F32) * mm(a_ref[...], w_ref[...])
            out = y if out is None else out + y
        o_ref[...] = out.astype(o_ref.dtype)
        return
    acc = mm(a_refs[0][...], w_refs[0][...])
    if epi == "cast":
        o_ref[...] = acc.astype(o_ref.dtype)
    elif epi == "sigmoid":
        o_ref[...] = _sigmoid(acc).astype(o_ref.dtype)
    elif epi == "resid":
        o_ref[...] = (extra[0][...] + acc).astype(o_ref.dtype)
    elif epi == "gnorm":
        gain = extra[0][...]
        for c0 in range(0, acc.shape[1], group):
            y = acc[:, c0:c0 + group]
            ms = jnp.mean(y * y, axis=-1, keepdims=True)
            o_ref[:, c0:c0 + group] = (y * lax.rsqrt(ms + EPS) * gain[:, c0:c0 + group]).astype(o_ref.dtype)
    else:
        raise ValueError(epi)


def _mm_wcast_kernel(*refs, n_a, n_w, epi, group, w_rows, shift, precast):
    a_refs = refs[:n_a]
    w_refs = refs[n_a:n_a + n_w]
    if precast:
        _mm_body(a_refs, w_refs, refs[n_a + n_w:-1], refs[-1], epi, group, w_rows)
        return
    n_next = 1 if shift else 0
    next_refs = refs[n_a + n_w:n_a + n_w + n_next]
    extra = refs[n_a + n_w + n_next:-1 - n_w]
    o_ref = refs[-1 - n_w]
    wbf_refs = refs[-n_w:]

    @pl.when(pl.program_id(1) == 0)
    def _():
        if shift:
            tn = w_refs[0].shape[0]
            wbf_refs[0][0:tn - shift, :] = w_refs[0][shift:tn, :].astype(BF16)
            wbf_refs[0][tn - shift:tn, :] = next_refs[0][...].astype(BF16)
        else:
            for w_ref, wbf_ref in zip(w_refs, wbf_refs):
                wbf_ref[...] = w_ref[...].astype(BF16)

    _mm_body(a_refs, wbf_refs, extra, o_ref, epi, group, w_rows)


def _matmul_wcast(a_list, w_list, w_maps, extras, extra_specs, *, n, tm, tn, out_dtype, epi, name, group=0,
                  w_rows=False, shift=0, next_map=None, precast=False):
    m = a_list[0].shape[0]
    in_specs = [pl.BlockSpec((tm, a.shape[1]), lambda j, i: (i, 0)) for a in a_list]
    w_blocks = [(tn, w.shape[2]) if w_rows else (w.shape[1], tn) for w in w_list]
    in_specs += [pl.BlockSpec((None,) + blk, wm) for blk, wm in zip(w_blocks, w_maps)]
    operands = list(a_list) + list(w_list)
    if shift:
        assert w_rows and len(w_list) == 1 and shift % 16 == 0 and tn % shift == 0
        in_specs.append(pl.BlockSpec((None, shift, w_list[0].shape[2]), next_map))
        operands.append(w_list[0])
    in_specs += list(extra_specs)
    kern = functools.partial(_mm_wcast_kernel, n_a=len(a_list), n_w=len(w_list), epi=epi, group=group,
                             w_rows=w_rows, shift=shift, precast=precast)
    return pl.pallas_call(
        kern,
        grid=(n // tn, m // tm),
        in_specs=in_specs,
        out_specs=pl.BlockSpec((tm, tn), lambda j, i: (i, j)),
        out_shape=jax.ShapeDtypeStruct((m, n), out_dtype),
        scratch_shapes=[] if precast else [pltpu.VMEM(blk, BF16) for blk in w_blocks],
        compiler_params=_cparams("parallel", "arbitrary"),
        name=name,
    )(*operands, *extras)


def _gla_kernel(q_ref, k_ref, v_ref, r_ref, h_ref, wa1_ref, wa2_ref, ba_ref, gn_ref, o_ref, s_ref, wa1_bf_ref, *, bt):
    c_len = GLA_CHUNK

    @pl.when(pl.program_id(0) == 0)
    def _():
        s_ref[...] = jnp.zeros_like(s_ref)
        wa1_bf_ref[...] = wa1_ref[...].astype(BF16)

    row = lax.broadcasted_iota(jnp.int32, (bt, bt), 0)
    col = lax.broadcasted_iota(jnp.int32, (bt, bt), 1)
    shift = c_len.bit_length() - 1
    same_chunk = (row >> shift) == (col >> shift)
    tri = jnp.logical_and(col <= row, same_chunk).astype(BF16)
    crow = lax.broadcasted_iota(jnp.int32, (c_len, c_len), 0)
    ccol = lax.broadcasted_iota(jnp.int32, (c_len, c_len), 1)
    causal = ccol <= crow

    a1 = _dot_nt(h_ref[...], wa1_bf_ref[...]).astype(BF16)
    pre = _dot(a1, wa2_ref[...]) + ba_ref[...]
    log_a = _softplus2(pre * -LOG2_E) * (-1.0 / GLA_TAU)
    la_hi, la_lo = _split_bf16(log_a)
    b = _dot(tri, la_hi) + _dot(tri, la_lo)

    gn = gn_ref[...]
    scale = GLA_DK ** -0.5
    for c in range(bt // c_len):
        rs = slice(c * c_len, (c + 1) * c_len)
        for h in range(GLA_HEADS):
            ks = slice(h * GLA_DK, (h + 1) * GLA_DK)
            vs = slice(h * GLA_DV, (h + 1) * GLA_DV)
            bh = b[rs, ks]
            b_last = bh[c_len - 1:c_len, :]
            k = k_ref[rs, ks]
            v = v_ref[rs, vs].astype(BF16)
            q_e = (q_ref[rs, ks] * (scale * jnp.exp2(bh))).astype(BF16)
            k_e = (k * jnp.exp2(-bh)).astype(BF16)
            k_d = (k * jnp.exp2(b_last - bh)).astype(BF16)
            s = jnp.where(causal, _dot_nt(q_e, k_e), 0.0).astype(BF16)
            state_t = s_ref[h]
            o = _dot(s, v) + _dot_nt(q_e, state_t.astype(BF16))
            s_ref[h] = state_t * jnp.exp2(b_last) + lax.dot_general(v, k_d, _TN, preferred_element_type=F32)
            ms = jnp.mean(o * o, axis=-1, keepdims=True)
            o_n = o * lax.rsqrt(ms + EPS) * gn
            r = r_ref[rs, vs]
            o_ref[rs, vs] = (o_n * (r * _sigmoid(r))).astype(o_ref.dtype)


def _gla(p_gla, h, w_in_t, a1_block, wa2p, ba, gn, *, bt):
    t, d = h.shape
    hk = GLA_HEADS * GLA_DK
    hv = GLA_HEADS * GLA_DV
    return pl.pallas_call(
        functools.partial(_gla_kernel, bt=bt),
        grid=(t // bt,),
        in_specs=[
            pl.BlockSpec((bt, hk), lambda i: (i, 0)),
            pl.BlockSpec((bt, hk), lambda i: (i, 1)),
            pl.BlockSpec((bt, hv), lambda i: (i, 1)),
            pl.BlockSpec((bt, hv), lambda i: (i, 2)),
            pl.BlockSpec((bt, d), lambda i: (i, 0)),
            pl.BlockSpec((None, LANES, d), lambda i: a1_block),
            pl.BlockSpec((LANES, hk), lambda i: (0, 0)),
            pl.BlockSpec((1, hk), lambda i: (0, 0)),
            pl.BlockSpec((1, GLA_DV), lambda i: (0, 0)),
        ],
        out_specs=pl.BlockSpec((bt, hv), lambda i: (i, 0)),
        out_shape=jax.ShapeDtypeStruct((t, hv), BF16),
        scratch_shapes=[pltpu.VMEM((GLA_HEADS, GLA_DV, GLA_DK), F32), pltpu.VMEM((LANES, d), BF16)],
        compiler_params=_cparams("arbitrary"),
        name="gla",
    )(p_gla, p_gla, p_gla, p_gla, h, w_in_t, wa2p, ba.reshape(1, hk), gn.reshape(1, GLA_DV))


def _sb_kernel(q_ref, k_ref, v_ref, o_ref, *, tile, heads):
    i = pl.program_id(1)
    row = lax.broadcasted_iota(jnp.int32, (tile, tile), 0)
    col = lax.broadcasted_iota(jnp.int32, (tile, tile), 1)
    from_here = (row >= col).astype(BF16)

    def step_head(hd, j, c, acc, diag):
        cs = slice(hd * SB_DH, (hd + 1) * SB_DH)
        start = pl.multiple_of(j * tile, tile)
        q = q_ref[:, cs]
        k = k_ref[pl.ds(start, tile), cs]
        v = v_ref[pl.ds(start, tile), cs]
        z = _dot_nt(q, k)
        l = _softplus2(z)
        if diag:
            causal = col < row
            l = jnp.where(causal, l, 0.0)
        e = z - _dot(l.astype(BF16), from_here)
        if c is not None:
            e = e - c
        a = jnp.exp2(e)
        if diag:
            a = jnp.where(causal, a, 0.0)
        pv = _dot(a.astype(BF16), v)
        tot = jnp.sum(l, axis=-1, keepdims=True)
        if c is None:
            return tot, pv
        return c + tot, acc + pv

    def step(j, state, diag):
        if state is None:
            state = [(None, None)] * heads
        return tuple(step_head(hd, j, c, acc, diag) for hd, (c, acc) in enumerate(state))

    def only_diag():
        return step(i, None, True)

    def diag_and_previous():
        return step(i - 1, step(i, None, True), False)

    state = lax.cond(i == 0, only_diag, diag_and_previous)

    def cond(carry):
        j, state = carry
        c_min = functools.reduce(jnp.minimum, [jnp.min(c) for c, _ in state])
        return jnp.logical_and(j >= 0, c_min <= F32_EXP2_UNDERFLOW)

    def body(carry):
        j, state = carry
        return j - 1, step(j, state, False)

    _, state = lax.while_loop(cond, body, (i - 2, state))
    for hd, (_, acc) in enumerate(state):
        o_ref[:, hd * SB_DH:(hd + 1) * SB_DH] = acc.astype(o_ref.dtype)


def _sb_attention(qk, v, *, tile, heads):
    t = v.shape[0]
    groups = SB_HEADS // heads
    w = heads * SB_DH
    return pl.pallas_call(
        functools.partial(_sb_kernel, tile=tile, heads=heads),
        grid=(groups, t // tile),
        in_specs=[
            pl.BlockSpec((tile, w), lambda g, i: (i, g)),
            pl.BlockSpec((t, w), lambda g, i: (0, groups + g)),
            pl.BlockSpec((t, w), lambda g, i: (0, g)),
        ],
        out_specs=pl.BlockSpec((tile, w), lambda g, i: (i, g)),
        out_shape=jax.ShapeDtypeStruct((t, SB_HEADS * SB_DH), BF16),
        compiler_params=_cparams("parallel", "parallel"),
        name="sb_attention",
    )(qk, qk, v)


def _mem_kernel(q_ref, k_ref, v_ref, o_ref):
    scale = MEM_DH ** -0.5
    for h in range(MEM_HEADS):
        cs = slice(h * MEM_DH, (h + 1) * MEM_DH)
        s = _dot_nt(q_ref[:, cs], k_ref[:, cs]) * scale
        e = jnp.exp(s - jnp.max(s, axis=-1, keepdims=True))
        p = e / jnp.sum(e, axis=-1, keepdims=True)
        o_ref[:, cs] = _dot(p.astype(BF16), v_ref[:, cs]).astype(o_ref.dtype)


def _mem_attention(q, mk, mv, *, tq):
    t, w = q.shape
    m = mk.shape[0]
    return pl.pallas_call(
        _mem_kernel,
        grid=(t // tq,),
        in_specs=[
            pl.BlockSpec((tq, w), lambda i: (i, 0)),
            pl.BlockSpec((m, w), lambda i: (0, 0)),
            pl.BlockSpec((m, w), lambda i: (0, 0)),
        ],
        out_specs=pl.BlockSpec((tq, w), lambda i: (i, 0)),
        out_shape=jax.ShapeDtypeStruct((t, w), BF16),
        compiler_params=_cparams("parallel"),
        name="mem_attention",
    )(q, mk, mv)


PROJ_TN = 1024


def _w_in_segments(d):
    hk = GLA_HEADS * GLA_DK
    hv = GLA_HEADS * GLA_DV
    mix = SB_HEADS * SB_DH
    sizes = {"gla": 2 * hk + 2 * hv, "a1": GLA_RANK, "sb_qk": 2 * mix, "sb_v": mix, "mem_q": MEM_HEADS * MEM_DH,
             "gates": N_BRANCH * d}
    offs, src = {}, 0
    for name, size in sizes.items():
        offs[name] = src
        src += size
    return offs


def _layer(x, mem, p, layer):
    t, d = x.shape
    m = mem.shape[0]
    hk = GLA_HEADS * GLA_DK
    hv = GLA_HEADS * GLA_DV
    mix = SB_HEADS * SB_DH
    mw = MEM_HEADS * MEM_DH
    d_ff = p["w_down"].shape[1]
    seg = _w_in_segments(d)

    def wcol(off=0):
        return lambda j, i: (layer, 0, j + off)

    def proj(a, name, n, out_dtype, epi, *, tn=PROJ_TN, group=0, extras=(), extra_specs=()):
        shift = seg[name] % tn
        base = seg[name] - shift
        rows = lambda j, i: (layer, base // tn + j, 0)
        next_map = (lambda j, i: (layer, (base + (j + 1) * tn) // shift, 0)) if shift else None
        return _matmul_wcast([a], [p["w_in_t"]], [rows], list(extras), list(extra_specs), n=n, tm=1024, tn=tn,
                             out_dtype=out_dtype, epi=epi, group=group, name="proj_" + name, w_rows=True,
                             shift=shift, next_map=next_map)

    def tile_spec(tm, tn, off=0):
        return pl.BlockSpec((tm, tn), lambda j, i: (i, j + off))

    gain_spec = pl.BlockSpec((1, PROJ_TN), lambda j, i: (0, j))

    h = _rmsnorm(x, p["attn_norm"][layer], tm=512)

    p_gla = proj(h, "gla", 2 * hk + 2 * hv, F32, "cast")
    assert seg["a1"] % LANES == 0
    wa2p = jnp.pad(p["gla_w_a2"][layer], ((0, LANES - GLA_RANK), (0, 0))).astype(BF16)
    o_gla = _gla(p_gla, h, p["w_in_t"], (layer, seg["a1"] // LANES, 0), wa2p, p["gla_b_a"][layer],
                 p["gla_out_norm"][layer], bt=512)

    q_gain = p["sb_q_norm"][layer] * (SB_DH ** -0.5 * LOG2_E)
    qk_gain = jnp.concatenate([jnp.tile(q_gain, SB_HEADS), jnp.tile(p["sb_k_norm"][layer], SB_HEADS)]).reshape(1, -1)
    qk = proj(h, "sb_qk", 2 * mix, BF16, "gnorm", group=SB_DH, extras=[qk_gain], extra_specs=[gain_spec])
    sv = proj(h, "sb_v", mix, BF16, "cast")
    o_sb = _sb_attention(qk, sv, tile=256, heads=2)

    hm = _rmsnorm(mem, p["mem_norm"][layer], tm=m)
    mk_gain = jnp.tile(p["mem_k_norm"][layer], MEM_HEADS).reshape(1, -1)
    mq_gain = jnp.tile(p["mem_q_norm"][layer], MEM_HEADS).reshape(1, -1)
    m_k = _matmul_wcast([hm], [p["w_mem_kv"]], [wcol()], [mk_gain], [gain_spec], n=mw, tm=m, tn=mw,
                        out_dtype=BF16, epi="gnorm", group=MEM_DH, name="proj_mem_k")
    m_v = _matmul_wcast([hm], [p["w_mem_kv"]], [wcol(1)], [], [], n=mw, tm=m, tn=mw, out_dtype=BF16, epi="cast",
                        name="proj_mem_v")
    q_m = proj(h, "mem_q", mw, BF16, "gnorm", group=MEM_DH, extras=[mq_gain], extra_specs=[gain_spec])
    o_mem = _mem_attention(q_m, m_k, m_v, tq=512)

    gates = proj(h, "gates", N_BRANCH * d, BF16, "sigmoid")
    tm, tn = 512, 1024
    nj = d // tn
    merged = _matmul_wcast(
        [o_gla, o_sb, o_mem], [p["w_br_gla"], p["w_br_sb"], p["w_br_mem"]], [wcol(), wcol(), wcol()],
        [gates, gates, gates], [tile_spec(tm, tn, b * nj) for b in range(N_BRANCH)],
        n=d, tm=tm, tn=tn, out_dtype=BF16, epi="merge", name="branch_merge")
    tm, tn = 1024, 1024
    x = _matmul_wcast([merged], [p["w_o"]], [wcol()], [x], [tile_spec(tm, tn)],
                      n=d, tm=tm, tn=tn, out_dtype=F32, epi="resid", name="out_proj")

    h2 = _rmsnorm(x, p["ffn_norm"][layer], tm=512)
    tm, tn = 1024, 512
    act = _matmul_wcast([h2], [p["w_gate_up"], p["w_gate_up"]], [wcol(), wcol(d_ff // tn)], [], [],
                        n=d_ff, tm=tm, tn=tn, out_dtype=BF16, epi="swiglu", name="ffn_gate_up")
    tm, tn = 512, 1024
    x = _matmul_wcast([act], [p["w_down_bf16"]], [wcol()], [x], [tile_spec(tm, tn)],
                      n=d, tm=tm, tn=tn, out_dtype=F32, epi="resid", name="ffn_down", precast=True)
    return x


_PARAM_NAMES = ("attn_norm", "w_in", "gla_w_a2", "gla_b_a", "gla_out_norm", "w_br_gla", "sb_q_norm", "sb_k_norm",
                "w_br_sb", "mem_norm", "w_mem_kv", "mem_q_norm", "mem_k_norm", "w_br_mem", "w_o", "ffn_norm",
                "w_gate_up", "w_down")


def kernel(x, mem, attn_norm, w_in, gla_w_a2, gla_b_a, gla_out_norm, w_br_gla, sb_q_norm, sb_k_norm, w_br_sb,
           mem_norm, w_mem_kv, mem_q_norm, mem_k_norm, w_br_mem, w_o, ffn_norm, w_gate_up, w_down):
    params = dict(zip(_PARAM_NAMES, (attn_norm, w_in, gla_w_a2, gla_b_a, gla_out_norm, w_br_gla, sb_q_norm,
                                     sb_k_norm, w_br_sb, mem_norm, w_mem_kv, mem_q_norm, mem_k_norm, w_br_mem,
                                     w_o, ffn_norm, w_gate_up, w_down)))
    b, t, d = x.shape
    assert b == 1, "kernels are written for a single sequence"
    xs = x.reshape(t, d)
    ms = mem.reshape(mem.shape[1], d)
    params["w_in_t"] = jnp.swapaxes(w_in, 1, 2)
    params["w_down_bf16"] = _cast_bf16(w_down, tk=512)
    for layer in range(w_in.shape[0]):
        xs = _layer(xs, ms, params, layer)
    return xs.reshape(b, t, d)
```

```python
import functools
import math

import jax
import jax.numpy as jnp
from jax import lax
from jax.experimental import pallas as pl
from jax.experimental.pallas import tpu as pltpu

F32 = jnp.float32
BF16 = jnp.bfloat16
EPS = 1e-6

GLA_HEADS = 4
GLA_DK = 128
GLA_DV = 256
GLA_RANK = 16
GLA_TAU = 16.0
GLA_CHUNK = 64
GLA_CUMSUM_ROWS = 256
SB_HEADS = 8
SB_DH = 128
MEM_HEADS = 4
MEM_DH = 256
N_BRANCH = 3

LANES = 128
VMEM_LIMIT_BYTES = 56 * 2**20
F32_EXP2_UNDERFLOW = 150.0
LOG2_E = math.log2(math.e)

_NT = (((1,), (1,)), ((), ()))
_TN = (((0,), (0,)), ((), ()))


def _dot(a, b):
    return jnp.dot(a, b, preferred_element_type=F32)


def _dot_nt(a, b):
    return lax.dot_general(a, b, _NT, preferred_element_type=F32)


def _softplus2(z2):
    return jnp.where(z2 > 64.0, z2, jnp.log2(1.0 + jnp.exp2(z2)))


def _sigmoid(z):
    return 0.5 * jnp.tanh(0.5 * z) + 0.5


def _split_bf16(v):
    hi = v.astype(BF16)
    lo = (v - hi.astype(F32)).astype(BF16)
    return hi, lo


def _cparams(*sem):
    return pltpu.CompilerParams(dimension_semantics=sem, vmem_limit_bytes=VMEM_LIMIT_BYTES)


def _rmsnorm_kernel(x_ref, g_ref, o_ref):
    x = x_ref[...]
    ms = jnp.mean(x * x, axis=-1, keepdims=True)
    o_ref[...] = (x * lax.rsqrt(ms + EPS) * g_ref[...]).astype(o_ref.dtype)


def _rmsnorm(x, g, *, tm):
    m, d = x.shape
    return pl.pallas_call(
        _rmsnorm_kernel,
        grid=(m // tm,),
        in_specs=[pl.BlockSpec((tm, d), lambda i: (i, 0)), pl.BlockSpec((1, d), lambda i: (0, 0))],
        out_specs=pl.BlockSpec((tm, d), lambda i: (i, 0)),
        out_shape=jax.ShapeDtypeStruct((m, d), BF16),
        compiler_params=_cparams("parallel"),
        name="rmsnorm",
    )(x, g.reshape(1, d))


def _cast_kernel(w_ref, o_ref):
    o_ref[...] = w_ref[...].astype(o_ref.dtype)


def _cast_bf16(w, *, tk):
    n_l, k, n = w.shape
    return pl.pallas_call(
        _cast_kernel,
        grid=(n_l, k // tk),
        in_specs=[pl.BlockSpec((None, tk, n), lambda l, i: (l, i, 0))],
        out_specs=pl.BlockSpec((None, tk, n), lambda l, i: (l, i, 0)),
        out_shape=jax.ShapeDtypeStruct(w.shape, BF16),
        compiler_params=_cparams("parallel", "parallel"),
        name="cast_bf16",
    )(w)


def _mm_body(a_refs, w_refs, extra, o_ref, epi, group, w_rows):
    mm = _dot_nt if w_rows else _dot
    if epi == "swiglu":
        a = a_refs[0][...]
        gate = mm(a, w_refs[0][...])
        up = mm(a, w_refs[1][...])
        o_ref[...] = (gate * _sigmoid(gate) * up).astype(o_ref.dtype)
        return
    if epi == "merge":
        out = None
        for a_ref, w_ref, g_ref in zip(a_refs, w_refs, extra):
            y = g_ref[...].astype(F32) * mm(a_ref[...], w_ref[...])
            out = y if out is None else out + y
        o_ref[...] = out.astype(o_ref.dtype)
        return
    acc = mm(a_refs[0][...], w_refs[0][...])
    if epi == "cast":
        o_ref[...] = acc.astype(o_ref.dtype)
    elif epi == "sigmoid":
        o_ref[...] = _sigmoid(acc).astype(o_ref.dtype)
    elif epi == "resid":
        o_ref[...] = (extra[0][...] + acc).astype(o_ref.dtype)
    elif epi == "gnorm":
        gain = extra[0][...]
        for c0 in range(0, acc.shape[1], group):
            y = acc[:, c0:c0 + group]
            ms = jnp.mean(y * y, axis=-1, keepdims=True)
            o_ref[:, c0:c0 + group] = (y * lax.rsqrt(ms + EPS) * gain[:, c0:c0 + group]).astype(o_ref.dtype)
    else:
        raise ValueError(epi)


def _mm_wcast_kernel(*refs, n_a, n_w, epi, group, w_rows, shift, precast):
    a_refs = refs[:n_a]
    w_refs = refs[n_a:n_a + n_w]
    if precast:
        _mm_body(a_refs, w_refs, refs[n_a + n_w:-1], refs[-1], epi, group, w_rows)
        return
    n_next = 1 if shift else 0
    next_refs = refs[n_a + n_w:n_a + n_w + n_next]
    extra = refs[n_a + n_w + n_next:-1 - n_w]
    o_ref = refs[-1 - n_w]
    wbf_refs = refs[-n_w:]

    @pl.when(pl.program_id(1) == 0)
    def _():
        if shift:
            tn = w_refs[0].shape[0]
            wbf_refs[0][0:tn - shift, :] = w_refs[0][shift:tn, :].astype(BF16)
            wbf_refs[0][tn - shift:tn, :] = next_refs[0][...].astype(BF16)
        else:
            for w_ref, wbf_ref in zip(w_refs, wbf_refs):
                wbf_ref[...] = w_ref[...].astype(BF16)

    _mm_body(a_refs, wbf_refs, extra, o_ref, epi, group, w_rows)


def _matmul_wcast(a_list, w_list, w_maps, extras, extra_specs, *, n, tm, tn, out_dtype, epi, name, group=0,
                  w_rows=False, shift=0, next_map=None, precast=False):
    m = a_list[0].shape[0]
    in_specs = [pl.BlockSpec((tm, a.shape[1]), lambda j, i: (i, 0)) for a in a_list]
    w_blocks = [(tn, w.shape[2]) if w_rows else (w.shape[1], tn) for w in w_list]
    in_specs += [pl.BlockSpec((None,) + blk, wm) for blk, wm in zip(w_blocks, w_maps)]
    operands = list(a_list) + list(w_list)
    if shift:
        assert w_rows and len(w_list) == 1 and shift % 16 == 0 and tn % shift == 0
        in_specs.append(pl.BlockSpec((None, shift, w_list[0].shape[2]), next_map))
        operands.append(w_list[0])
    in_specs += list(extra_specs)
    kern = functools.partial(_mm_wcast_kernel, n_a=len(a_list), n_w=len(w_list), epi=epi, group=group,
                             w_rows=w_rows, shift=shift, precast=precast)
    return pl.pallas_call(
        kern,
        grid=(n // tn, m // tm),
        in_specs=in_specs,
        out_specs=pl.BlockSpec((tm, tn), lambda j, i: (i, j)),
        out_shape=jax.ShapeDtypeStruct((m, n), out_dtype),
        scratch_shapes=[] if precast else [pltpu.VMEM(blk, BF16) for blk in w_blocks],
        compiler_params=_cparams("parallel", "arbitrary"),
        name=name,
    )(*operands, *extras)


def _gla_kernel(q_ref, k_ref, v_ref, r_ref, h_ref, wa1_ref, wa2_ref, ba_ref, gn_ref, o_ref, s_ref, wa1_bf_ref, *, bt):
    c_len = GLA_CHUNK

    @pl.when(pl.program_id(0) == 0)
    def _():
        s_ref[...] = jnp.zeros_like(s_ref)
        wa1_bf_ref[...] = wa1_ref[...].astype(BF16)

    cum = min(bt, GLA_CUMSUM_ROWS)
    row = lax.broadcasted_iota(jnp.int32, (cum, cum), 0)
    col = lax.broadcasted_iota(jnp.int32, (cum, cum), 1)
    shift = c_len.bit_length() - 1
    same_chunk = (row >> shift) == (col >> shift)
    tri = jnp.logical_and(col <= row, same_chunk).astype(BF16)
    crow = lax.broadcasted_iota(jnp.int32, (c_len, c_len), 0)
    ccol = lax.broadcasted_iota(jnp.int32, (c_len, c_len), 1)
    causal = ccol <= crow

    a1 = _dot_nt(h_ref[...], wa1_bf_ref[...]).astype(BF16)
    pre = _dot(a1, wa2_ref[...]) + ba_ref[...]
    log_a = _softplus2(pre * -LOG2_E) * (-1.0 / GLA_TAU)
    la_hi, la_lo = _split_bf16(log_a)
    b = jnp.concatenate([_dot(tri, la_hi[r0:r0 + cum]) + _dot(tri, la_lo[r0:r0 + cum])
                         for r0 in range(0, bt, cum)], axis=0)

    gn = gn_ref[...]
    scale = GLA_DK ** -0.5
    for c in range(bt // c_len):
        rs = slice(c * c_len, (c + 1) * c_len)
        for h in range(GLA_HEADS):
            ks = slice(h * GLA_DK, (h + 1) * GLA_DK)
            vs = slice(h * GLA_DV, (h + 1) * GLA_DV)
            bh = b[rs, ks]
            b_last = bh[c_len - 1:c_len, :]
            k = k_ref[rs, ks]
            v = v_ref[rs, vs].astype(BF16)
            q_e = (q_ref[rs, ks] * (scale * jnp.exp2(bh))).astype(BF16)
            k_e = (k * jnp.exp2(-bh)).astype(BF16)
            k_d = (k * jnp.exp2(b_last - bh)).astype(BF16)
            s = jnp.where(causal, _dot_nt(q_e, k_e), 0.0).astype(BF16)
            state_t = s_ref[h]
            o = _dot(s, v) + _dot_nt(q_e, state_t.astype(BF16))
            s_ref[h] = state_t * jnp.exp2(b_last) + lax.dot_general(v, k_d, _TN, preferred_element_type=F32)
            ms = jnp.mean(o * o, axis=-1, keepdims=True)
            o_n = o * lax.rsqrt(ms + EPS) * gn
            r = r_ref[rs, vs]
            o_ref[rs, vs] = (o_n * (r * _sigmoid(r))).astype(o_ref.dtype)


def _gla(p_gla, h, w_in_t, a1_block, wa2p, ba, gn, *, bt):
    t, d = h.shape
    hk = GLA_HEADS * GLA_DK
    hv = GLA_HEADS * GLA_DV
    return pl.pallas_call(
        functools.partial(_gla_kernel, bt=bt),
        grid=(t // bt,),
        in_specs=[
            pl.BlockSpec((bt, hk), lambda i: (i, 0)),
            pl.BlockSpec((bt, hk), lambda i: (i, 1)),
            pl.BlockSpec((bt, hv), lambda i: (i, 1)),
            pl.BlockSpec((bt, hv), lambda i: (i, 2)),
            pl.BlockSpec((bt, d), lambda i: (i, 0)),
            pl.BlockSpec((None, LANES, d), lambda i: a1_block),
            pl.BlockSpec((LANES, hk), lambda i: (0, 0)),
            pl.BlockSpec((1, hk), lambda i: (0, 0)),
            pl.BlockSpec((1, GLA_DV), lambda i: (0, 0)),
        ],
        out_specs=pl.BlockSpec((bt, hv), lambda i: (i, 0)),
        out_shape=jax.ShapeDtypeStruct((t, hv), BF16),
        scratch_shapes=[pltpu.VMEM((GLA_HEADS, GLA_DV, GLA_DK), F32), pltpu.VMEM((LANES, d), BF16)],
        compiler_params=_cparams("arbitrary"),
        name="gla",
    )(p_gla, p_gla, p_gla, p_gla, h, w_in_t, wa2p, ba.reshape(1, hk), gn.reshape(1, GLA_DV))


def _sb_kernel(q_ref, k_ref, v_ref, o_ref, *, tile, heads, n_sub):
    i = pl.program_id(1)
    row = lax.broadcasted_iota(jnp.int32, (tile, tile), 0)
    col = lax.broadcasted_iota(jnp.int32, (tile, tile), 1)
    from_here = (row >= col).astype(BF16)
    causal = col < row
    units = [(hd, sub) for hd in range(heads) for sub in range(n_sub)]

    def tile_step(hd, sub, j, c, acc, diag, guard):
        cs = slice(hd * SB_DH, (hd + 1) * SB_DH)
        start = pl.multiple_of((jnp.maximum(j, 0) if guard else j) * tile, tile)
        q = q_ref[sub * tile:(sub + 1) * tile, cs]
        k = k_ref[pl.ds(start, tile), cs]
        v = v_ref[pl.ds(start, tile), cs]
        z = _dot_nt(q, k)
        l = _softplus2(z)
        keep = causal if diag else (jnp.broadcast_to(j, (tile, tile)) >= 0 if guard else None)
        if keep is not None:
            l = jnp.where(keep, l, 0.0)
        e = z - _dot(l.astype(BF16), from_here)
        if c is not None:
            e = e - c
        a = jnp.exp2(e)
        if keep is not None:
            a = jnp.where(keep, a, 0.0)
        pv = _dot(a.astype(BF16), v)
        tot = jnp.sum(l, axis=-1, keepdims=True)
        if c is None:
            return tot, pv
        return c + tot, acc + pv

    def static_part(first_step):
        out = []
        for hd, sub in units:
            g = i * n_sub + sub
            st = tile_step(hd, sub, g, None, None, True, False)
            if not (first_step and sub == 0):
                st = tile_step(hd, sub, g - 1, *st, False, False)
            out.append(st)
        return tuple(out)

    state = lax.cond(i == 0, lambda: static_part(True), lambda: static_part(False))

    def cond(carry):
        n, state = carry
        c_min = functools.reduce(jnp.minimum, [jnp.min(c) for c, _ in state])
        j_newest = i * n_sub + (n_sub - 1) - 2 - n
        return jnp.logical_and(j_newest >= 0, c_min <= F32_EXP2_UNDERFLOW)

    def body(carry):
        n, state = carry
        new = tuple(tile_step(hd, sub, i * n_sub + sub - 2 - n, c, acc, False, True)
                    for (hd, sub), (c, acc) in zip(units, state))
        return n + 1, new

    _, state = lax.while_loop(cond, body, (jnp.int32(0), state))
    for (hd, sub), (_, acc) in zip(units, state):
        o_ref[sub * tile:(sub + 1) * tile, hd * SB_DH:(hd + 1) * SB_DH] = acc.astype(o_ref.dtype)


def _sb_attention(qk, v, *, tile, heads, n_sub):
    t = v.shape[0]
    groups = SB_HEADS // heads
    w = heads * SB_DH
    rows = tile * n_sub
    return pl.pallas_call(
        functools.partial(_sb_kernel, tile=tile, heads=heads, n_sub=n_sub),
        grid=(groups, t // rows),
        in_specs=[
            pl.BlockSpec((rows, w), lambda g, i: (i, g)),
            pl.BlockSpec((t, w), lambda g, i: (0, groups + g)),
            pl.BlockSpec((t, w), lambda g, i: (0, g)),
        ],
        out_specs=pl.BlockSpec((rows, w), lambda g, i: (i, g)),
        out_shape=jax.ShapeDtypeStruct((t, SB_HEADS * SB_DH), BF16),
        compiler_params=_cparams("parallel", "parallel"),
        name="sb_attention",
    )(qk, qk, v)


def _mem_kernel(q_ref, k_ref, v_ref, o_ref):
    for h in range(MEM_HEADS):
        cs = slice(h * MEM_DH, (h + 1) * MEM_DH)
        s = _dot_nt(q_ref[:, cs], k_ref[:, cs])
        e = jnp.exp2(s - jnp.max(s, axis=-1, keepdims=True))
        p = e / jnp.sum(e, axis=-1, keepdims=True)
        o_ref[:, cs] = _dot(p.astype(BF16), v_ref[:, cs]).astype(o_ref.dtype)


def _mem_attention(q, mk, mv, *, tq):
    t, w = q.shape
    m = mk.shape[0]
    return pl.pallas_call(
        _mem_kernel,
        grid=(t // tq,),
        in_specs=[
            pl.BlockSpec((tq, w), lambda i: (i, 0)),
            pl.BlockSpec((m, w), lambda i: (0, 0)),
            pl.BlockSpec((m, w), lambda i: (0, 0)),
        ],
        out_specs=pl.BlockSpec((tq, w), lambda i: (i, 0)),
        out_shape=jax.ShapeDtypeStruct((t, w), BF16),
        compiler_params=_cparams("parallel"),
        name="mem_attention",
    )(q, mk, mv)


PROJ_TN = 1024


def _w_in_segments(d):
    hk = GLA_HEADS * GLA_DK
    hv = GLA_HEADS * GLA_DV
    mix = SB_HEADS * SB_DH
    sizes = {"gla": 2 * hk + 2 * hv, "a1": GLA_RANK, "sb_qk": 2 * mix, "sb_v": mix, "mem_q": MEM_HEADS * MEM_DH,
             "gates": N_BRANCH * d}
    offs, src = {}, 0
    for name, size in sizes.items():
        offs[name] = src
        src += size
    return offs


def _layer(x, mem, p, layer):
    t, d = x.shape
    m = mem.shape[0]
    hk = GLA_HEADS * GLA_DK
    hv = GLA_HEADS * GLA_DV
    mix = SB_HEADS * SB_DH
    mw = MEM_HEADS * MEM_DH
    d_ff = p["w_down"].shape[1]
    seg = _w_in_segments(d)

    def wcol(off=0):
        return lambda j, i: (layer, 0, j + off)

    def proj(a, name, n, out_dtype, epi, *, tn=PROJ_TN, group=0, extras=(), extra_specs=()):
        shift = seg[name] % tn
        base = seg[name] - shift
        rows = lambda j, i: (layer, base // tn + j, 0)
        next_map = (lambda j, i: (layer, (base + (j + 1) * tn) // shift, 0)) if shift else None
        return _matmul_wcast([a], [p["w_in_t"]], [rows], list(extras), list(extra_specs), n=n, tm=1024, tn=tn,
                             out_dtype=out_dtype, epi=epi, group=group, name="proj_" + name, w_rows=True,
                             shift=shift, next_map=next_map)

    def tile_spec(tm, tn, off=0):
        return pl.BlockSpec((tm, tn), lambda j, i: (i, j + off))

    gain_spec = pl.BlockSpec((1, PROJ_TN), lambda j, i: (0, j))

    h = _rmsnorm(x, p["attn_norm"][layer], tm=512)

    p_gla = proj(h, "gla", 2 * hk + 2 * hv, F32, "cast")
    assert seg["a1"] % LANES == 0
    wa2p = jnp.pad(p["gla_w_a2"][layer], ((0, LANES - GLA_RANK), (0, 0))).astype(BF16)
    o_gla = _gla(p_gla, h, p["w_in_t"], (layer, seg["a1"] // LANES, 0), wa2p, p["gla_b_a"][layer],
                 p["gla_out_norm"][layer], bt=512)

    q_gain = p["sb_q_norm"][layer] * (SB_DH ** -0.5 * LOG2_E)
    qk_gain = jnp.concatenate([jnp.tile(q_gain, SB_HEADS), jnp.tile(p["sb_k_norm"][layer], SB_HEADS)]).reshape(1, -1)
    qk = proj(h, "sb_qk", 2 * mix, BF16, "gnorm", group=SB_DH, extras=[qk_gain], extra_specs=[gain_spec])
    sv = proj(h, "sb_v", mix, BF16, "cast")
    o_sb = _sb_attention(qk, sv, tile=256, heads=2, n_sub=2)

    hm = _rmsnorm(mem, p["mem_norm"][layer], tm=m)
    mk_gain = jnp.tile(p["mem_k_norm"][layer], MEM_HEADS).reshape(1, -1)
    mq_gain = jnp.tile(p["mem_q_norm"][layer] * (MEM_DH ** -0.5 * LOG2_E), MEM_HEADS).reshape(1, -1)
    m_k = _matmul_wcast([hm], [p["w_mem_kv"]], [wcol()], [mk_gain], [gain_spec], n=mw, tm=m, tn=mw,
                        out_dtype=BF16, epi="gnorm", group=MEM_DH, name="proj_mem_k")
    m_v = _matmul_wcast([hm], [p["w_mem_kv"]], [wcol(1)], [], [], n=mw, tm=m, tn=mw, out_dtype=BF16, epi="cast",
                        name="proj_mem_v")
    q_m = proj(h, "mem_q", mw, BF16, "gnorm", group=MEM_DH, extras=[mq_gain], extra_specs=[gain_spec])
    o_mem = _mem_attention(q_m, m_k, m_v, tq=1024)

    gates = proj(h, "gates", N_BRANCH * d, BF16, "sigmoid")
    tm, tn = 512, 1024
    nj = d // tn
    merged = _matmul_wcast(
        [o_gla, o_sb, o_mem], [p["w_br_gla"], p["w_br_sb"], p["w_br_mem"]], [wcol(), wcol(), wcol()],
        [gates, gates, gates], [tile_spec(tm, tn, b * nj) for b in range(N_BRANCH)],
        n=d, tm=tm, tn=tn, out_dtype=BF16, epi="merge", name="branch_merge")
    tm, tn = 1024, 1024
    x = _matmul_wcast([merged], [p["w_o"]], [wcol()], [x], [tile_spec(tm, tn)],
                      n=d, tm=tm, tn=tn, out_dtype=F32, epi="resid", name="out_proj")

    h2 = _rmsnorm(x, p["ffn_norm"][layer], tm=512)
    tm, tn = 1024, 512
    act = _matmul_wcast([h2], [p["w_gate_up"], p["w_gate_up"]], [wcol(), wcol(d_ff // tn)], [], [],
                        n=d_ff, tm=tm, tn=tn, out_dtype=BF16, epi="swiglu", name="ffn_gate_up")
    tm, tn = 512, 1024
    x = _matmul_wcast([act], [p["w_down_bf16"]], [wcol()], [x], [tile_spec(tm, tn)],
                      n=d, tm=tm, tn=tn, out_dtype=F32, epi="resid", name="ffn_down", precast=True)
    return x


_PARAM_NAMES = ("attn_norm", "w_in", "gla_w_a2", "gla_b_a", "gla_out_norm", "w_br_gla", "sb_q_norm", "sb_k_norm",
                "w_br_sb", "mem_norm", "w_mem_kv", "mem_q_norm", "mem_k_norm", "w_br_mem", "w_o", "ffn_norm",
                "w_gate_up", "w_down")


def kernel(x, mem, attn_norm, w_in, gla_w_a2, gla_b_a, gla_out_norm, w_br_gla, sb_q_norm, sb_k_norm, w_br_sb,
           mem_norm, w_mem_kv, mem_q_norm, mem_k_norm, w_br_mem, w_o, ffn_norm, w_gate_up, w_down):
    params = dict(zip(_PARAM_NAMES, (attn_norm, w_in, gla_w_a2, gla_b_a, gla_out_norm, w_br_gla, sb_q_norm,
                                     sb_k_norm, w_br_sb, mem_norm, w_mem_kv, mem_q_norm, mem_k_norm, w_br_mem,
                                     w_o, ffn_norm, w_gate_up, w_down)))
    b, t, d = x.shape
    assert b == 1, "kernels are written for a single sequence"
    xs = x.reshape(t, d)
    ms = mem.reshape(mem.shape[1], d)
    params["w_in_t"] = jnp.swapaxes(w_in, 1, 2)
    params["w_down_bf16"] = _cast_bf16(w_down, tk=512)
    for layer in range(w_in.shape[0]):
        xs = _layer(xs, ms, params, layer)
    return xs.reshape(b, t, d)
```

```python
import functools
import math

import jax
import jax.numpy as jnp
from jax import lax
from jax.experimental import pallas as pl
from jax.experimental.pallas import tpu as pltpu

F32 = jnp.float32
BF16 = jnp.bfloat16
EPS = 1e-6

GLA_HEADS = 4
GLA_DK = 128
GLA_DV = 256
GLA_RANK = 16
GLA_TAU = 16.0
GLA_CHUNK = 64
GLA_CUMSUM_ROWS = 256
SB_HEADS = 8
SB_DH = 128
MEM_HEADS = 4
MEM_DH = 256
N_BRANCH = 3

LANES = 128
VMEM_LIMIT_BYTES = 56 * 2**20
F32_EXP2_UNDERFLOW = 150.0
LOG2_E = math.log2(math.e)

_NT = (((1,), (1,)), ((), ()))
_TN = (((0,), (0,)), ((), ()))


def _dot(a, b):
    return jnp.dot(a, b, preferred_element_type=F32)


def _dot_nt(a, b):
    return lax.dot_general(a, b, _NT, preferred_element_type=F32)


def _softplus2(z2):
    return jnp.where(z2 > 64.0, z2, jnp.log2(1.0 + jnp.exp2(z2)))


def _sigmoid(z):
    return 0.5 * jnp.tanh(0.5 * z) + 0.5


def _split_bf16(v):
    hi = v.astype(BF16)
    lo = (v - hi.astype(F32)).astype(BF16)
    return hi, lo


def _cparams(*sem):
    return pltpu.CompilerParams(dimension_semantics=sem, vmem_limit_bytes=VMEM_LIMIT_BYTES)


def _rmsnorm_kernel(x_ref, g_ref, o_ref):
    x = x_ref[...]
    ms = jnp.mean(x * x, axis=-1, keepdims=True)
    o_ref[...] = (x * lax.rsqrt(ms + EPS) * g_ref[...]).astype(o_ref.dtype)


def _rmsnorm(x, g, *, tm):
    m, d = x.shape
    return pl.pallas_call(
        _rmsnorm_kernel,
        grid=(m // tm,),
        in_specs=[pl.BlockSpec((tm, d), lambda i: (i, 0)), pl.BlockSpec((1, d), lambda i: (0, 0))],
        out_specs=pl.BlockSpec((tm, d), lambda i: (i, 0)),
        out_shape=jax.ShapeDtypeStruct((m, d), BF16),
        compiler_params=_cparams("parallel"),
        name="rmsnorm",
    )(x, g.reshape(1, d))


def _cast_kernel(w_ref, o_ref):
    o_ref[...] = w_ref[...].astype(o_ref.dtype)


def _cast_bf16(w, *, tk):
    n_l, k, n = w.shape
    return pl.pallas_call(
        _cast_kernel,
        grid=(n_l, k // tk),
        in_specs=[pl.BlockSpec((None, tk, n), lambda l, i: (l, i, 0))],
        out_specs=pl.BlockSpec((None, tk, n), lambda l, i: (l, i, 0)),
        out_shape=jax.ShapeDtypeStruct(w.shape, BF16),
        compiler_params=_cparams("parallel", "parallel"),
        name="cast_bf16",
    )(w)


def _mm_body(a_refs, w_refs, extra, o_ref, epi, group, w_rows):
    mm = _dot_nt if w_rows else _dot
    if epi == "swiglu":
        a = a_refs[0][...]
        gate = mm(a, w_refs[0][...])
        up = mm(a, w_refs[1][...])
        o_ref[...] = (gate * _sigmoid(gate) * up).astype(o_ref.dtype)
        return
    if epi == "merge":
        out = None
        for a_ref, w_ref, g_ref in zip(a_refs, w_refs, extra):
            y = g_ref[...].astype(F32) * mm(a_ref[...], w_ref[...])
            out = y if out is None else out + y
        o_ref[...] = out.astype(o_ref.dtype)
        return
    acc = mm(a_refs[0][...], w_refs[0][...])
    if epi == "cast":
        o_ref[...] = acc.astype(o_ref.dtype)
    elif epi == "sigmoid":
        o_ref[...] = _sigmoid(acc).astype(o_ref.dtype)
    elif epi == "resid":
        o_ref[...] = (extra[0][...] + acc).astype(o_ref.dtype)
    elif epi == "gnorm":
        gain = extra[0][...]
        for c0 in range(0, acc.shape[1], group):
            y = acc[:, c0:c0 + group]
            ms = jnp.mean(y * y, axis=-1, keepdims=True)
            o_ref[:, c0:c0 + group] = (y * lax.rsqrt(ms + EPS) * gain[:, c0:c0 + group]).astype(o_ref.dtype)
    else:
        raise ValueError(epi)


def _mm_wcast_kernel(*refs, n_a, n_w, epi, group, w_rows, shift, precast):
    a_refs = refs[:n_a]
    w_refs = refs[n_a:n_a + n_w]
    if precast:
        _mm_body(a_refs, w_refs, refs[n_a + n_w:-1], refs[-1], epi, group, w_rows)
        return
    n_next = 1 if shift else 0
    next_refs = refs[n_a + n_w:n_a + n_w + n_next]
    extra = refs[n_a + n_w + n_next:-1 - n_w]
    o_ref = refs[-1 - n_w]
    wbf_refs = refs[-n_w:]

    @pl.when(pl.program_id(1) == 0)
    def _():
        if shift:
            tn = w_refs[0].shape[0]
            wbf_refs[0][0:tn - shift, :] = w_refs[0][shift:tn, :].astype(BF16)
            wbf_refs[0][tn - shift:tn, :] = next_refs[0][...].astype(BF16)
        else:
            for w_ref, wbf_ref in zip(w_refs, wbf_refs):
                wbf_ref[...] = w_ref[...].astype(BF16)

    _mm_body(a_refs, wbf_refs, extra, o_ref, epi, group, w_rows)


def _matmul_wcast(a_list, w_list, w_maps, extras, extra_specs, *, n, tm, tn, out_dtype, epi, name, group=0,
                  w_rows=False, shift=0, next_map=None, precast=False):
    m = a_list[0].shape[0]
    in_specs = [pl.BlockSpec((tm, a.shape[1]), lambda j, i: (i, 0)) for a in a_list]
    w_blocks = [(tn, w.shape[2]) if w_rows else (w.shape[1], tn) for w in w_list]
    in_specs += [pl.BlockSpec((None,) + blk, wm) for blk, wm in zip(w_blocks, w_maps)]
    operands = list(a_list) + list(w_list)
    if shift:
        assert w_rows and len(w_list) == 1 and shift % 16 == 0 and tn % shift == 0
        in_specs.append(pl.BlockSpec((None, shift, w_list[0].shape[2]), next_map))
        operands.append(w_list[0])
    in_specs += list(extra_specs)
    kern = functools.partial(_mm_wcast_kernel, n_a=len(a_list), n_w=len(w_list), epi=epi, group=group,
                             w_rows=w_rows, shift=shift, precast=precast)
    return pl.pallas_call(
        kern,
        grid=(n // tn, m // tm),
        in_specs=in_specs,
        out_specs=pl.BlockSpec((tm, tn), lambda j, i: (i, j)),
        out_shape=jax.ShapeDtypeStruct((m, n), out_dtype),
        scratch_shapes=[] if precast else [pltpu.VMEM(blk, BF16) for blk in w_blocks],
        compiler_params=_cparams("parallel", "arbitrary"),
        name=name,
    )(*operands, *extras)


def _gla_kernel(q_ref, k_ref, v_ref, r_ref, h_ref, wa1_ref, wa2_ref, ba_ref, gn_ref, o_ref, s_ref, wa1_bf_ref, *, bt):
    c_len = GLA_CHUNK

    @pl.when(pl.program_id(0) == 0)
    def _():
        s_ref[...] = jnp.zeros_like(s_ref)
        wa1_bf_ref[...] = wa1_ref[...].astype(BF16)

    cum = min(bt, GLA_CUMSUM_ROWS)
    row = lax.broadcasted_iota(jnp.int32, (cum, cum), 0)
    col = lax.broadcasted_iota(jnp.int32, (cum, cum), 1)
    shift = c_len.bit_length() - 1
    same_chunk = (row >> shift) == (col >> shift)
    tri = jnp.logical_and(col <= row, same_chunk).astype(BF16)
    crow = lax.broadcasted_iota(jnp.int32, (c_len, c_len), 0)
    ccol = lax.broadcasted_iota(jnp.int32, (c_len, c_len), 1)
    causal = ccol <= crow

    a1 = _dot_nt(h_ref[...], wa1_bf_ref[...]).astype(BF16)
    pre = _dot(a1, wa2_ref[...]) + ba_ref[...]
    log_a = _softplus2(pre * -LOG2_E) * (-1.0 / GLA_TAU)
    la_hi, la_lo = _split_bf16(log_a)
    b = jnp.concatenate([_dot(tri, la_hi[r0:r0 + cum]) + _dot(tri, la_lo[r0:r0 + cum])
                         for r0 in range(0, bt, cum)], axis=0)

    gn = gn_ref[...]
    scale = GLA_DK ** -0.5
    n_chunks = bt // c_len
    units = [(c, h) for c in range(n_chunks) for h in range(GLA_HEADS)]
    prep = {}
    for c, h in units:
        rs = slice(c * c_len, (c + 1) * c_len)
        ks = slice(h * GLA_DK, (h + 1) * GLA_DK)
        vs = slice(h * GLA_DV, (h + 1) * GLA_DV)
        bh = b[rs, ks]
        b_last = bh[c_len - 1:c_len, :]
        k = k_ref[rs, ks]
        v = v_ref[rs, vs].astype(BF16)
        q_e = (q_ref[rs, ks] * (scale * jnp.exp2(bh))).astype(BF16)
        k_e = (k * jnp.exp2(-bh)).astype(BF16)
        k_d = (k * jnp.exp2(b_last - bh)).astype(BF16)
        prep[c, h] = (q_e, k_e, k_d, v, jnp.exp2(b_last))
    scores = {u: jnp.where(causal, _dot_nt(prep[u][0], prep[u][1]), 0.0).astype(BF16) for u in units}
    pre_s = {}
    for u in units:
        q_e, _, k_d, v, decay = prep[u]
        d_state = lax.dot_general(v, k_d, _TN, preferred_element_type=F32)
        pre_s[u] = (q_e, _dot(scores[u], v), d_state, decay)
    for h in range(GLA_HEADS):
        vs = slice(h * GLA_DV, (h + 1) * GLA_DV)
        state_t = s_ref[h]
        for c in range(n_chunks):
            rs = slice(c * c_len, (c + 1) * c_len)
            q_e, o_intra, d_state, decay = pre_s[c, h]
            o = o_intra + _dot_nt(q_e, state_t.astype(BF16))
            state_t = state_t * decay + d_state
            ms = jnp.mean(o * o, axis=-1, keepdims=True)
            o_n = o * lax.rsqrt(ms + EPS) * gn
            r = r_ref[rs, vs]
            o_ref[rs, vs] = (o_n * (r * _sigmoid(r))).astype(o_ref.dtype)
        s_ref[h] = state_t


def _gla(p_gla, h, w_in_t, a1_block, wa2p, ba, gn, *, bt):
    t, d = h.shape
    hk = GLA_HEADS * GLA_DK
    hv = GLA_HEADS * GLA_DV
    return pl.pallas_call(
        functools.partial(_gla_kernel, bt=bt),
        grid=(t // bt,),
        in_specs=[
            pl.BlockSpec((bt, hk), lambda i: (i, 0)),
            pl.BlockSpec((bt, hk), lambda i: (i, 1)),
            pl.BlockSpec((bt, hv), lambda i: (i, 1)),
            pl.BlockSpec((bt, hv), lambda i: (i, 2)),
            pl.BlockSpec((bt, d), lambda i: (i, 0)),
            pl.BlockSpec((None, LANES, d), lambda i: a1_block),
            pl.BlockSpec((LANES, hk), lambda i: (0, 0)),
            pl.BlockSpec((1, hk), lambda i: (0, 0)),
            pl.BlockSpec((1, GLA_DV), lambda i: (0, 0)),
        ],
        out_specs=pl.BlockSpec((bt, hv), lambda i: (i, 0)),
        out_shape=jax.ShapeDtypeStruct((t, hv), BF16),
        scratch_shapes=[pltpu.VMEM((GLA_HEADS, GLA_DV, GLA_DK), F32), pltpu.VMEM((LANES, d), BF16)],
        compiler_params=_cparams("arbitrary"),
        name="gla",
    )(p_gla, p_gla, p_gla, p_gla, h, w_in_t, wa2p, ba.reshape(1, hk), gn.reshape(1, GLA_DV))


def _sb_kernel(q_ref, k_ref, v_ref, o_ref, *, tile, heads, n_sub):
    i = pl.program_id(1)
    row = lax.broadcasted_iota(jnp.int32, (tile, tile), 0)
    col = lax.broadcasted_iota(jnp.int32, (tile, tile), 1)
    from_here = (row >= col).astype(BF16)
    causal = col < row
    units = [(hd, sub) for hd in range(heads) for sub in range(n_sub)]

    def tile_step(hd, sub, j, c, acc, diag, guard):
        cs = slice(hd * SB_DH, (hd + 1) * SB_DH)
        start = pl.multiple_of((jnp.maximum(j, 0) if guard else j) * tile, tile)
        q = q_ref[sub * tile:(sub + 1) * tile, cs]
        k = k_ref[pl.ds(start, tile), cs]
        v = v_ref[pl.ds(start, tile), cs]
        z = _dot_nt(q, k)
        l = _softplus2(z)
        keep = causal if diag else (jnp.broadcast_to(j, (tile, tile)) >= 0 if guard else None)
        if keep is not None:
            l = jnp.where(keep, l, 0.0)
        e = z - _dot(l.astype(BF16), from_here)
        if c is not None:
            e = e - c
        a = jnp.exp2(e)
        if keep is not None:
            a = jnp.where(keep, a, 0.0)
        pv = _dot(a.astype(BF16), v)
        tot = jnp.sum(l, axis=-1, keepdims=True)
        if c is None:
            return tot, pv
        return c + tot, acc + pv

    def static_part(first_step):
        jobs = []
        for u, (hd, sub) in enumerate(units):
            g = i * n_sub + sub
            jobs.append((u, g, True))
            if not (first_step and sub == 0):
                jobs.append((u, g - 1, False))

        def operands(u, j):
            hd, sub = units[u]
            cs = slice(hd * SB_DH, (hd + 1) * SB_DH)
            start = pl.multiple_of(j * tile, tile)
            return q_ref[sub * tile:(sub + 1) * tile, cs], k_ref[pl.ds(start, tile), cs], v_ref[pl.ds(start, tile), cs]

        jobs.sort(key=lambda job: not job[2])
        zs = [_dot_nt(*operands(u, j)[:2]) for u, j, _ in jobs]
        ls = [jnp.where(causal, _softplus2(z), 0.0) if diag else _softplus2(z) for z, (_, _, diag) in zip(zs, jobs)]
        ws = [_dot(l.astype(BF16), from_here) for l in ls]
        tots = [jnp.sum(l, axis=-1, keepdims=True) for l in ls]
        state = [None] * len(units)
        for (u, j, diag), z, w, tot in zip(jobs, zs, ws, tots):
            v = operands(u, j)[2]
            if diag:
                a = jnp.where(causal, jnp.exp2(z - w), 0.0)
                state[u] = (tot, _dot(a.astype(BF16), v))
            else:
                c, acc = state[u]
                a = jnp.exp2(z - w - c)
                state[u] = (c + tot, acc + _dot(a.astype(BF16), v))
        return tuple(state)

    state = lax.cond(i == 0, lambda: static_part(True), lambda: static_part(False))

    def cond(carry):
        n, state = carry
        c_min = functools.reduce(jnp.minimum, [jnp.min(c) for c, _ in state])
        j_newest = i * n_sub + (n_sub - 1) - 2 - n
        return jnp.logical_and(j_newest >= 0, c_min <= F32_EXP2_UNDERFLOW)

    def body(carry):
        n, state = carry
        new = tuple(tile_step(hd, sub, i * n_sub + sub - 2 - n, c, acc, False, True)
                    for (hd, sub), (c, acc) in zip(units, state))
        return n + 1, new

    _, state = lax.while_loop(cond, body, (jnp.int32(0), state))
    for (hd, sub), (_, acc) in zip(units, state):
        o_ref[sub * tile:(sub + 1) * tile, hd * SB_DH:(hd + 1) * SB_DH] = acc.astype(o_ref.dtype)


def _sb_attention(qk, v, *, tile, heads, n_sub):
    t = v.shape[0]
    groups = SB_HEADS // heads
    w = heads * SB_DH
    rows = tile * n_sub
    return pl.pallas_call(
        functools.partial(_sb_kernel, tile=tile, heads=heads, n_sub=n_sub),
        grid=(groups, t // rows),
        in_specs=[
            pl.BlockSpec((rows, w), lambda g, i: (i, g)),
            pl.BlockSpec((t, w), lambda g, i: (0, groups + g)),
            pl.BlockSpec((t, w), lambda g, i: (0, g)),
        ],
        out_specs=pl.BlockSpec((rows, w), lambda g, i: (i, g)),
        out_shape=jax.ShapeDtypeStruct((t, SB_HEADS * SB_DH), BF16),
        compiler_params=_cparams("parallel", "parallel"),
        name="sb_attention",
    )(qk, qk, v)


def _mem_kernel(q_ref, k_ref, v_ref, o_ref):
    for h in range(MEM_HEADS):
        cs = slice(h * MEM_DH, (h + 1) * MEM_DH)
        s = _dot_nt(q_ref[:, cs], k_ref[:, cs])
        e = jnp.exp2(s - jnp.max(s, axis=-1, keepdims=True))
        p = e / jnp.sum(e, axis=-1, keepdims=True)
        o_ref[:, cs] = _dot(p.astype(BF16), v_ref[:, cs]).astype(o_ref.dtype)


def _mem_attention(q, mk, mv, *, tq):
    t, w = q.shape
    m = mk.shape[0]
    return pl.pallas_call(
        _mem_kernel,
        grid=(t // tq,),
        in_specs=[
            pl.BlockSpec((tq, w), lambda i: (i, 0)),
            pl.BlockSpec((m, w), lambda i: (0, 0)),
            pl.BlockSpec((m, w), lambda i: (0, 0)),
        ],
        out_specs=pl.BlockSpec((tq, w), lambda i: (i, 0)),
        out_shape=jax.ShapeDtypeStruct((t, w), BF16),
        compiler_params=_cparams("parallel"),
        name="mem_attention",
    )(q, mk, mv)


PROJ_TN = 1024


def _w_in_segments(d):
    hk = GLA_HEADS * GLA_DK
    hv = GLA_HEADS * GLA_DV
    mix = SB_HEADS * SB_DH
    sizes = {"gla": 2 * hk + 2 * hv, "a1": GLA_RANK, "sb_qk": 2 * mix, "sb_v": mix, "mem_q": MEM_HEADS * MEM_DH,
             "gates": N_BRANCH * d}
    offs, src = {}, 0
    for name, size in sizes.items():
        offs[name] = src
        src += size
    return offs


def _layer(x, mem, p, layer):
    t, d = x.shape
    m = mem.shape[0]
    hk = GLA_HEADS * GLA_DK
    hv = GLA_HEADS * GLA_DV
    mix = SB_HEADS * SB_DH
    mw = MEM_HEADS * MEM_DH
    d_ff = p["w_down"].shape[1]
    seg = _w_in_segments(d)

    def wcol(off=0):
        return lambda j, i: (layer, 0, j + off)

    def proj(a, name, n, out_dtype, epi, *, tn=PROJ_TN, group=0, extras=(), extra_specs=()):
        shift = seg[name] % tn
        base = seg[name] - shift
        rows = lambda j, i: (layer, base // tn + j, 0)
        next_map = (lambda j, i: (layer, (base + (j + 1) * tn) // shift, 0)) if shift else None
        return _matmul_wcast([a], [p["w_in_t"]], [rows], list(extras), list(extra_specs), n=n, tm=1024, tn=tn,
                             out_dtype=out_dtype, epi=epi, group=group, name="proj_" + name, w_rows=True,
                             shift=shift, next_map=next_map)

    def tile_spec(tm, tn, off=0):
        return pl.BlockSpec((tm, tn), lambda j, i: (i, j + off))

    gain_spec = pl.BlockSpec((1, PROJ_TN), lambda j, i: (0, j))

    h = _rmsnorm(x, p["attn_norm"][layer], tm=512)

    p_gla = proj(h, "gla", 2 * hk + 2 * hv, F32, "cast")
    assert seg["a1"] % LANES == 0
    wa2p = jnp.pad(p["gla_w_a2"][layer], ((0, LANES - GLA_RANK), (0, 0))).astype(BF16)
    o_gla = _gla(p_gla, h, p["w_in_t"], (layer, seg["a1"] // LANES, 0), wa2p, p["gla_b_a"][layer],
                 p["gla_out_norm"][layer], bt=512)

    q_gain = p["sb_q_norm"][layer] * (SB_DH ** -0.5 * LOG2_E)
    qk_gain = jnp.concatenate([jnp.tile(q_gain, SB_HEADS), jnp.tile(p["sb_k_norm"][layer], SB_HEADS)]).reshape(1, -1)
    qk = proj(h, "sb_qk", 2 * mix, BF16, "gnorm", group=SB_DH, extras=[qk_gain], extra_specs=[gain_spec])
    sv = proj(h, "sb_v", mix, BF16, "cast")
    o_sb = _sb_attention(qk, sv, tile=256, heads=2, n_sub=2)

    hm = _rmsnorm(mem, p["mem_norm"][layer], tm=m)
    mk_gain = jnp.tile(p["mem_k_norm"][layer], MEM_HEADS).reshape(1, -1)
    mq_gain = jnp.tile(p["mem_q_norm"][layer] * (MEM_DH ** -0.5 * LOG2_E), MEM_HEADS).reshape(1, -1)
    m_k = _matmul_wcast([hm], [p["w_mem_kv"]], [wcol()], [mk_gain], [gain_spec], n=mw, tm=m, tn=mw,
                        out_dtype=BF16, epi="gnorm", group=MEM_DH, name="proj_mem_k")
    m_v = _matmul_wcast([hm], [p["w_mem_kv"]], [wcol(1)], [], [], n=mw, tm=m, tn=mw, out_dtype=BF16, epi="cast",
                        name="proj_mem_v")
    q_m = proj(h, "mem_q", mw, BF16, "gnorm", group=MEM_DH, extras=[mq_gain], extra_specs=[gain_spec])
    o_mem = _mem_attention(q_m, m_k, m_v, tq=1024)

    gates = proj(h, "gates", N_BRANCH * d, BF16, "sigmoid")
    tm, tn = 512, 1024
    nj = d // tn
    merged = _matmul_wcast(
        [o_gla, o_sb, o_mem], [p["w_br_gla"], p["w_br_sb"], p["w_br_mem"]], [wcol(), wcol(), wcol()],
        [gates, gates, gates], [tile_spec(tm, tn, b * nj) for b in range(N_BRANCH)],
        n=d, tm=tm, tn=tn, out_dtype=BF16, epi="merge", name="branch_merge")
    tm, tn = 1024, 1024
    x = _matmul_wcast([merged], [p["w_o"]], [wcol()], [x], [tile_spec(tm, tn)],
                      n=d, tm=tm, tn=tn, out_dtype=F32, epi="resid", name="out_proj")

    h2 = _rmsnorm(x, p["ffn_norm"][layer], tm=512)
    tm, tn = 1024, 512
    act = _matmul_wcast([h2], [p["w_gate_up"], p["w_gate_up"]], [wcol(), wcol(d_ff // tn)], [], [],
                        n=d_ff, tm=tm, tn=tn, out_dtype=BF16, epi="swiglu", name="ffn_gate_up")
    tm, tn = 512, 1024
    x = _matmul_wcast([act], [p["w_down_bf16"]], [wcol()], [x], [tile_spec(tm, tn)],
                      n=d, tm=tm, tn=tn, out_dtype=F32, epi="resid", name="ffn_down", precast=True)
    return x


_PARAM_NAMES = ("attn_norm", "w_in", "gla_w_a2", "gla_b_a", "gla_out_norm", "w_br_gla", "sb_q_norm", "sb_k_norm",
                "w_br_sb", "mem_norm", "w_mem_kv", "mem_q_norm", "mem_k_norm", "w_br_mem", "w_o", "ffn_norm",
                "w_gate_up", "w_down")


def kernel(x, mem, attn_norm, w_in, gla_w_a2, gla_b_a, gla_out_norm, w_br_gla, sb_q_norm, sb_k_norm, w_br_sb,
           mem_norm, w_mem_kv, mem_q_norm, mem_k_norm, w_br_mem, w_o, ffn_norm, w_gate_up, w_down):
    params = dict(zip(_PARAM_NAMES, (attn_norm, w_in, gla_w_a2, gla_b_a, gla_out_norm, w_br_gla, sb_q_norm,
                                     sb_k_norm, w_br_sb, mem_norm, w_mem_kv, mem_q_norm, mem_k_norm, w_br_mem,
                                     w_o, ffn_norm, w_gate_up, w_down)))
    b, t, d = x.shape
    assert b == 1, "kernels are written for a single sequence"
    xs = x.reshape(t, d)
    ms = mem.reshape(mem.shape[1], d)
    params["w_in_t"] = jnp.swapaxes(w_in, 1, 2)
    params["w_down_bf16"] = _cast_bf16(w_down, tk=512)
    for layer in range(w_in.shape[0]):
        xs = _layer(xs, ms, params, layer)
    return xs.reshape(b, t, d)
```

```python
import functools
import math

import jax
import jax.numpy as jnp
from jax import lax
from jax.experimental import pallas as pl
from jax.experimental.pallas import tpu as pltpu

F32 = jnp.float32
BF16 = jnp.bfloat16
EPS = 1e-6

GLA_HEADS = 4
GLA_DK = 128
GLA_DV = 256
GLA_RANK = 16
GLA_TAU = 16.0
GLA_CHUNK = 64
GLA_CUMSUM_ROWS = 256
SB_HEADS = 8
SB_DH = 128
MEM_HEADS = 4
MEM_DH = 256
N_BRANCH = 3

LANES = 128
VMEM_LIMIT_BYTES = 56 * 2**20
F32_EXP2_UNDERFLOW = 150.0
LOG2_E = math.log2(math.e)

_NT = (((1,), (1,)), ((), ()))
_TN = (((0,), (0,)), ((), ()))


def _dot(a, b):
    return jnp.dot(a, b, preferred_element_type=F32)


def _dot_nt(a, b):
    return lax.dot_general(a, b, _NT, preferred_element_type=F32)


def _softplus2(z2):
    return jnp.where(z2 > 64.0, z2, jnp.log2(1.0 + jnp.exp2(z2)))


def _sigmoid(z):
    return 0.5 * jnp.tanh(0.5 * z) + 0.5


def _split_bf16(v):
    hi = v.astype(BF16)
    lo = (v - hi.astype(F32)).astype(BF16)
    return hi, lo


def _cparams(*sem):
    return pltpu.CompilerParams(dimension_semantics=sem, vmem_limit_bytes=VMEM_LIMIT_BYTES)


def _rmsnorm_kernel(x_ref, g_ref, o_ref):
    x = x_ref[...]
    ms = jnp.mean(x * x, axis=-1, keepdims=True)
    o_ref[...] = (x * lax.rsqrt(ms + EPS) * g_ref[...]).astype(o_ref.dtype)


def _rmsnorm(x, g, *, tm):
    m, d = x.shape
    return pl.pallas_call(
        _rmsnorm_kernel,
        grid=(m // tm,),
        in_specs=[pl.BlockSpec((tm, d), lambda i: (i, 0)), pl.BlockSpec((1, d), lambda i: (0, 0))],
        out_specs=pl.BlockSpec((tm, d), lambda i: (i, 0)),
        out_shape=jax.ShapeDtypeStruct((m, d), BF16),
        compiler_params=_cparams("parallel"),
        name="rmsnorm",
    )(x, g.reshape(1, d))


def _cast_kernel(w_ref, o_ref):
    o_ref[...] = w_ref[...].astype(o_ref.dtype)


def _cast_bf16(w, *, tk):
    n_l, k, n = w.shape
    return pl.pallas_call(
        _cast_kernel,
        grid=(n_l, k // tk),
        in_specs=[pl.BlockSpec((None, tk, n), lambda l, i: (l, i, 0))],
        out_specs=pl.BlockSpec((None, tk, n), lambda l, i: (l, i, 0)),
        out_shape=jax.ShapeDtypeStruct(w.shape, BF16),
        compiler_params=_cparams("parallel", "parallel"),
        name="cast_bf16",
    )(w)


def _mm_body(a_refs, w_refs, extra, o_ref, epi, group, w_rows):
    mm = _dot_nt if w_rows else _dot
    if epi == "swiglu":
        a = a_refs[0][...]
        gate = mm(a, w_refs[0][...])
        up = mm(a, w_refs[1][...])
        o_ref[...] = (gate * _sigmoid(gate) * up).astype(o_ref.dtype)
        return
    if epi == "merge":
        out = None
        for a_ref, w_ref, g_ref in zip(a_refs, w_refs, extra):
            y = g_ref[...].astype(F32) * mm(a_ref[...], w_ref[...])
            out = y if out is None else out + y
        o_ref[...] = out.astype(o_ref.dtype)
        return
    acc = mm(a_refs[0][...], w_refs[0][...])
    if epi == "cast":
        o_ref[...] = acc.astype(o_ref.dtype)
    elif epi == "sigmoid":
        o_ref[...] = _sigmoid(acc).astype(o_ref.dtype)
    elif epi == "resid":
        o_ref[...] = (extra[0][...] + acc).astype(o_ref.dtype)
    elif epi == "gnorm":
        gain = extra[0][...]
        for c0 in range(0, acc.shape[1], group):
            y = acc[:, c0:c0 + group]
            ms = jnp.mean(y * y, axis=-1, keepdims=True)
            o_ref[:, c0:c0 + group] = (y * lax.rsqrt(ms + EPS) * gain[:, c0:c0 + group]).astype(o_ref.dtype)
    else:
        raise ValueError(epi)


def _mm_wcast_kernel(*refs, n_a, n_w, epi, group, w_rows, shift, precast):
    a_refs = refs[:n_a]
    w_refs = refs[n_a:n_a + n_w]
    if precast:
        _mm_body(a_refs, w_refs, refs[n_a + n_w:-1], refs[-1], epi, group, w_rows)
        return
    n_next = 1 if shift else 0
    next_refs = refs[n_a + n_w:n_a + n_w + n_next]
    extra = refs[n_a + n_w + n_next:-1 - n_w]
    o_ref = refs[-1 - n_w]
    wbf_refs = refs[-n_w:]

    @pl.when(pl.program_id(1) == 0)
    def _():
        if shift:
            tn = w_refs[0].shape[0]
            wbf_refs[0][0:tn - shift, :] = w_refs[0][shift:tn, :].astype(BF16)
            wbf_refs[0][tn - shift:tn, :] = next_refs[0][...].astype(BF16)
        else:
            for w_ref, wbf_ref in zip(w_refs, wbf_refs):
                wbf_ref[...] = w_ref[...].astype(BF16)

    _mm_body(a_refs, wbf_refs, extra, o_ref, epi, group, w_rows)


def _matmul_wcast(a_list, w_list, w_maps, extras, extra_specs, *, n, tm, tn, out_dtype, epi, name, group=0,
                  w_rows=False, shift=0, next_map=None, precast=False):
    m = a_list[0].shape[0]
    in_specs = [pl.BlockSpec((tm, a.shape[1]), lambda j, i: (i, 0)) for a in a_list]
    w_blocks = [(tn, w.shape[2]) if w_rows else (w.shape[1], tn) for w in w_list]
    in_specs += [pl.BlockSpec((None,) + blk, wm) for blk, wm in zip(w_blocks, w_maps)]
    operands = list(a_list) + list(w_list)
    if shift:
        assert w_rows and len(w_list) == 1 and shift % 16 == 0 and tn % shift == 0
        in_specs.append(pl.BlockSpec((None, shift, w_list[0].shape[2]), next_map))
        operands.append(w_list[0])
    in_specs += list(extra_specs)
    kern = functools.partial(_mm_wcast_kernel, n_a=len(a_list), n_w=len(w_list), epi=epi, group=group,
                             w_rows=w_rows, shift=shift, precast=precast)
    return pl.pallas_call(
        kern,
        grid=(n // tn, m // tm),
        in_specs=in_specs,
        out_specs=pl.BlockSpec((tm, tn), lambda j, i: (i, j)),
        out_shape=jax.ShapeDtypeStruct((m, n), out_dtype),
        scratch_shapes=[] if precast else [pltpu.VMEM(blk, BF16) for blk in w_blocks],
        compiler_params=_cparams("parallel", "arbitrary"),
        name=name,
    )(*operands, *extras)


def _resid_norm_kernel(a_ref, w_ref, x_ref, g_ref, xo_ref, ho_ref):
    x_new = x_ref[...] + _dot(a_ref[...], w_ref[...])
    xo_ref[...] = x_new
    ms = jnp.mean(x_new * x_new, axis=-1, keepdims=True)
    ho_ref[...] = (x_new * lax.rsqrt(ms + EPS) * g_ref[...]).astype(ho_ref.dtype)


def _matmul_resid_norm(a, w_bf16, layer, x, gain, *, tm):
    m, k = a.shape
    n = w_bf16.shape[2]
    return pl.pallas_call(
        _resid_norm_kernel,
        grid=(m // tm,),
        in_specs=[
            pl.BlockSpec((tm, k), lambda i: (i, 0)),
            pl.BlockSpec((None, k, n), lambda i: (layer, 0, 0)),
            pl.BlockSpec((tm, n), lambda i: (i, 0)),
            pl.BlockSpec((1, n), lambda i: (0, 0)),
        ],
        out_specs=[pl.BlockSpec((tm, n), lambda i: (i, 0)), pl.BlockSpec((tm, n), lambda i: (i, 0))],
        out_shape=[jax.ShapeDtypeStruct((m, n), F32), jax.ShapeDtypeStruct((m, n), BF16)],
        compiler_params=_cparams("parallel"),
        name="out_proj_norm",
    )(a, w_bf16, x, gain.reshape(1, n))


def _gla_kernel(q_ref, k_ref, v_ref, r_ref, h_ref, wa1_ref, wa2_ref, ba_ref, gn_ref, o_ref, s_ref, wa1_bf_ref, *, bt):
    c_len = GLA_CHUNK

    @pl.when(pl.program_id(0) == 0)
    def _():
        s_ref[...] = jnp.zeros_like(s_ref)
        wa1_bf_ref[...] = wa1_ref[...].astype(BF16)

    cum = min(bt, GLA_CUMSUM_ROWS)
    row = lax.broadcasted_iota(jnp.int32, (cum, cum), 0)
    col = lax.broadcasted_iota(jnp.int32, (cum, cum), 1)
    shift = c_len.bit_length() - 1
    same_chunk = (row >> shift) == (col >> shift)
    tri = jnp.logical_and(col <= row, same_chunk).astype(BF16)
    crow = lax.broadcasted_iota(jnp.int32, (c_len, c_len), 0)
    ccol = lax.broadcasted_iota(jnp.int32, (c_len, c_len), 1)
    causal = ccol <= crow

    a1 = _dot_nt(h_ref[...], wa1_bf_ref[...]).astype(BF16)
    pre = _dot(a1, wa2_ref[...]) + ba_ref[...]
    log_a = _softplus2(pre * -LOG2_E) * (-1.0 / GLA_TAU)
    la_hi, la_lo = _split_bf16(log_a)
    b = jnp.concatenate([_dot(tri, la_hi[r0:r0 + cum]) + _dot(tri, la_lo[r0:r0 + cum])
                         for r0 in range(0, bt, cum)], axis=0)

    gn = gn_ref[...]
    scale = GLA_DK ** -0.5
    n_chunks = bt // c_len
    units = [(c, h) for c in range(n_chunks) for h in range(GLA_HEADS)]
    prep = {}
    for c, h in units:
        rs = slice(c * c_len, (c + 1) * c_len)
        ks = slice(h * GLA_DK, (h + 1) * GLA_DK)
        vs = slice(h * GLA_DV, (h + 1) * GLA_DV)
        bh = b[rs, ks]
        b_last = bh[c_len - 1:c_len, :]
        k = k_ref[rs, ks]
        v = v_ref[rs, vs].astype(BF16)
        q_e = (q_ref[rs, ks] * (scale * jnp.exp2(bh))).astype(BF16)
        k_e = (k * jnp.exp2(-bh)).astype(BF16)
        k_d = (k * jnp.exp2(b_last - bh)).astype(BF16)
        prep[c, h] = (q_e, k_e, k_d, v, jnp.exp2(b_last))
    scores = {u: jnp.where(causal, _dot_nt(prep[u][0], prep[u][1]), 0.0).astype(BF16) for u in units}
    pre_s = {}
    for u in units:
        q_e, _, k_d, v, decay = prep[u]
        d_state = lax.dot_general(v, k_d, _TN, preferred_element_type=F32)
        pre_s[u] = (q_e, _dot(scores[u], v), d_state, decay)
    for h in range(GLA_HEADS):
        vs = slice(h * GLA_DV, (h + 1) * GLA_DV)
        state_t = s_ref[h]
        for c in range(n_chunks):
            rs = slice(c * c_len, (c + 1) * c_len)
            q_e, o_intra, d_state, decay = pre_s[c, h]
            o = o_intra + _dot_nt(q_e, state_t.astype(BF16))
            state_t = state_t * decay + d_state
            ms = jnp.mean(o * o, axis=-1, keepdims=True)
            o_n = o * lax.rsqrt(ms + EPS) * gn
            r = r_ref[rs, vs]
            o_ref[rs, vs] = (o_n * (r * _sigmoid(r))).astype(o_ref.dtype)
        s_ref[h] = state_t


def _gla(p_gla, h, w_in_t, a1_block, wa2p, ba, gn, *, bt):
    t, d = h.shape
    hk = GLA_HEADS * GLA_DK
    hv = GLA_HEADS * GLA_DV
    return pl.pallas_call(
        functools.partial(_gla_kernel, bt=bt),
        grid=(t // bt,),
        in_specs=[
            pl.BlockSpec((bt, hk), lambda i: (i, 0)),
            pl.BlockSpec((bt, hk), lambda i: (i, 1)),
            pl.BlockSpec((bt, hv), lambda i: (i, 1)),
            pl.BlockSpec((bt, hv), lambda i: (i, 2)),
            pl.BlockSpec((bt, d), lambda i: (i, 0)),
            pl.BlockSpec((None, LANES, d), lambda i: a1_block),
            pl.BlockSpec((LANES, hk), lambda i: (0, 0)),
            pl.BlockSpec((1, hk), lambda i: (0, 0)),
            pl.BlockSpec((1, GLA_DV), lambda i: (0, 0)),
        ],
        out_specs=pl.BlockSpec((bt, hv), lambda i: (i, 0)),
        out_shape=jax.ShapeDtypeStruct((t, hv), BF16),
        scratch_shapes=[pltpu.VMEM((GLA_HEADS, GLA_DV, GLA_DK), F32), pltpu.VMEM((LANES, d), BF16)],
        compiler_params=_cparams("arbitrary"),
        name="gla",
    )(p_gla, p_gla, p_gla, p_gla, h, w_in_t, wa2p, ba.reshape(1, hk), gn.reshape(1, GLA_DV))


def _sb_kernel(q_ref, k_ref, v_ref, o_ref, *, tile, heads, n_sub):
    i = pl.program_id(1)
    row = lax.broadcasted_iota(jnp.int32, (tile, tile), 0)
    col = lax.broadcasted_iota(jnp.int32, (tile, tile), 1)
    from_here = (row >= col).astype(BF16)
    causal = col < row
    units = [(hd, sub) for hd in range(heads) for sub in range(n_sub)]

    def tile_step(hd, sub, j, c, acc, diag, guard):
        cs = slice(hd * SB_DH, (hd + 1) * SB_DH)
        start = pl.multiple_of((jnp.maximum(j, 0) if guard else j) * tile, tile)
        q = q_ref[sub * tile:(sub + 1) * tile, cs]
        k = k_ref[pl.ds(start, tile), cs]
        v = v_ref[pl.ds(start, tile), cs]
        z = _dot_nt(q, k)
        l = _softplus2(z)
        keep = causal if diag else (jnp.broadcast_to(j, (tile, tile)) >= 0 if guard else None)
        if keep is not None:
            l = jnp.where(keep, l, 0.0)
        e = z - _dot(l.astype(BF16), from_here)
        if c is not None:
            e = e - c
        a = jnp.exp2(e)
        if keep is not None:
            a = jnp.where(keep, a, 0.0)
        pv = _dot(a.astype(BF16), v)
        tot = jnp.sum(l, axis=-1, keepdims=True)
        if c is None:
            return tot, pv
        return c + tot, acc + pv

    def static_part(first_step):
        jobs = []
        for u, (hd, sub) in enumerate(units):
            g = i * n_sub + sub
            jobs.append((u, g, True))
            if not (first_step and sub == 0):
                jobs.append((u, g - 1, False))

        def operands(u, j):
            hd, sub = units[u]
            cs = slice(hd * SB_DH, (hd + 1) * SB_DH)
            start = pl.multiple_of(j * tile, tile)
            return q_ref[sub * tile:(sub + 1) * tile, cs], k_ref[pl.ds(start, tile), cs], v_ref[pl.ds(start, tile), cs]

        jobs.sort(key=lambda job: not job[2])
        zs = [_dot_nt(*operands(u, j)[:2]) for u, j, _ in jobs]
        ls = [jnp.where(causal, _softplus2(z), 0.0) if diag else _softplus2(z) for z, (_, _, diag) in zip(zs, jobs)]
        ws = [_dot(l.astype(BF16), from_here) for l in ls]
        tots = [jnp.sum(l, axis=-1, keepdims=True) for l in ls]
        state = [None] * len(units)
        for (u, j, diag), z, w, tot in zip(jobs, zs, ws, tots):
            v = operands(u, j)[2]
            if diag:
                a = jnp.where(causal, jnp.exp2(z - w), 0.0)
                state[u] = (tot, _dot(a.astype(BF16), v))
            else:
                c, acc = state[u]
                a = jnp.exp2(z - w - c)
                state[u] = (c + tot, acc + _dot(a.astype(BF16), v))
        return tuple(state)

    state = lax.cond(i == 0, lambda: static_part(True), lambda: static_part(False))

    def cond(carry):
        n, state = carry
        c_min = functools.reduce(jnp.minimum, [jnp.min(c) for c, _ in state])
        j_newest = i * n_sub + (n_sub - 1) - 2 - n
        return jnp.logical_and(j_newest >= 0, c_min <= F32_EXP2_UNDERFLOW)

    def body(carry):
        n, state = carry
        new = tuple(tile_step(hd, sub, i * n_sub + sub - 2 - n, c, acc, False, True)
                    for (hd, sub), (c, acc) in zip(units, state))
        return n + 1, new

    _, state = lax.while_loop(cond, body, (jnp.int32(0), state))
    for (hd, sub), (_, acc) in zip(units, state):
        o_ref[sub * tile:(sub + 1) * tile, hd * SB_DH:(hd + 1) * SB_DH] = acc.astype(o_ref.dtype)


def _sb_attention(qk, v, *, tile, heads, n_sub):
    t = v.shape[0]
    groups = SB_HEADS // heads
    w = heads * SB_DH
    rows = tile * n_sub
    return pl.pallas_call(
        functools.partial(_sb_kernel, tile=tile, heads=heads, n_sub=n_sub),
        grid=(groups, t // rows),
        in_specs=[
            pl.BlockSpec((rows, w), lambda g, i: (i, g)),
            pl.BlockSpec((t, w), lambda g, i: (0, groups + g)),
            pl.BlockSpec((t, w), lambda g, i: (0, g)),
        ],
        out_specs=pl.BlockSpec((rows, w), lambda g, i: (i, g)),
        out_shape=jax.ShapeDtypeStruct((t, SB_HEADS * SB_DH), BF16),
        compiler_params=_cparams("parallel", "parallel"),
        name="sb_attention",
    )(qk, qk, v)


def _mem_kernel(q_ref, k_ref, v_ref, o_ref):
    cols = [slice(h * MEM_DH, (h + 1) * MEM_DH) for h in range(MEM_HEADS)]
    ss = [_dot_nt(q_ref[:, cs], k_ref[:, cs]) for cs in cols]
    es = [jnp.exp2(s - jnp.max(s, axis=-1, keepdims=True)) for s in ss]
    ps = [(e / jnp.sum(e, axis=-1, keepdims=True)).astype(BF16) for e in es]
    for cs, p in zip(cols, ps):
        o_ref[:, cs] = _dot(p, v_ref[:, cs]).astype(o_ref.dtype)


def _mem_attention(q, mk, mv, *, tq):
    t, w = q.shape
    m = mk.shape[0]
    return pl.pallas_call(
        _mem_kernel,
        grid=(t // tq,),
        in_specs=[
            pl.BlockSpec((tq, w), lambda i: (i, 0)),
            pl.BlockSpec((m, w), lambda i: (0, 0)),
            pl.BlockSpec((m, w), lambda i: (0, 0)),
        ],
        out_specs=pl.BlockSpec((tq, w), lambda i: (i, 0)),
        out_shape=jax.ShapeDtypeStruct((t, w), BF16),
        compiler_params=_cparams("parallel"),
        name="mem_attention",
    )(q, mk, mv)


PROJ_TN = 1024


def _w_in_segments(d):
    hk = GLA_HEADS * GLA_DK
    hv = GLA_HEADS * GLA_DV
    mix = SB_HEADS * SB_DH
    sizes = {"gla": 2 * hk + 2 * hv, "a1": GLA_RANK, "sb_qk": 2 * mix, "sb_v": mix, "mem_q": MEM_HEADS * MEM_DH,
             "gates": N_BRANCH * d}
    offs, src = {}, 0
    for name, size in sizes.items():
        offs[name] = src
        src += size
    return offs


def _layer(x, mem, p, layer):
    t, d = x.shape
    m = mem.shape[0]
    hk = GLA_HEADS * GLA_DK
    hv = GLA_HEADS * GLA_DV
    mix = SB_HEADS * SB_DH
    mw = MEM_HEADS * MEM_DH
    d_ff = p["w_down"].shape[1]
    seg = _w_in_segments(d)

    def wcol(off=0):
        return lambda j, i: (layer, 0, j + off)

    def proj(a, name, n, out_dtype, epi, *, tn=PROJ_TN, group=0, extras=(), extra_specs=()):
        shift = seg[name] % tn
        base = seg[name] - shift
        rows = lambda j, i: (layer, base // tn + j, 0)
        next_map = (lambda j, i: (layer, (base + (j + 1) * tn) // shift, 0)) if shift else None
        return _matmul_wcast([a], [p["w_in_t"]], [rows], list(extras), list(extra_specs), n=n, tm=1024, tn=tn,
                             out_dtype=out_dtype, epi=epi, group=group, name="proj_" + name, w_rows=True,
                             shift=shift, next_map=next_map)

    def tile_spec(tm, tn, off=0):
        return pl.BlockSpec((tm, tn), lambda j, i: (i, j + off))

    gain_spec = pl.BlockSpec((1, PROJ_TN), lambda j, i: (0, j))

    h = _rmsnorm(x, p["attn_norm"][layer], tm=512)

    p_gla = proj(h, "gla", 2 * hk + 2 * hv, F32, "cast")
    assert seg["a1"] % LANES == 0
    wa2p = jnp.pad(p["gla_w_a2"][layer], ((0, LANES - GLA_RANK), (0, 0))).astype(BF16)
    o_gla = _gla(p_gla, h, p["w_in_t"], (layer, seg["a1"] // LANES, 0), wa2p, p["gla_b_a"][layer],
                 p["gla_out_norm"][layer], bt=1024)

    q_gain = p["sb_q_norm"][layer] * (SB_DH ** -0.5 * LOG2_E)
    qk_gain = jnp.concatenate([jnp.tile(q_gain, SB_HEADS), jnp.tile(p["sb_k_norm"][layer], SB_HEADS)]).reshape(1, -1)
    qk = proj(h, "sb_qk", 2 * mix, BF16, "gnorm", group=SB_DH, extras=[qk_gain], extra_specs=[gain_spec])
    sv = proj(h, "sb_v", mix, BF16, "cast")
    o_sb = _sb_attention(qk, sv, tile=256, heads=2, n_sub=2)

    hm = _rmsnorm(mem, p["mem_norm"][layer], tm=m)
    mk_gain = jnp.tile(p["mem_k_norm"][layer], MEM_HEADS).reshape(1, -1)
    mq_gain = jnp.tile(p["mem_q_norm"][layer] * (MEM_DH ** -0.5 * LOG2_E), MEM_HEADS).reshape(1, -1)
    m_k = _matmul_wcast([hm], [p["w_mem_kv"]], [wcol()], [mk_gain], [gain_spec], n=mw, tm=m, tn=mw,
                        out_dtype=BF16, epi="gnorm", group=MEM_DH, name="proj_mem_k")
    m_v = _matmul_wcast([hm], [p["w_mem_kv"]], [wcol(1)], [], [], n=mw, tm=m, tn=mw, out_dtype=BF16, epi="cast",
                        name="proj_mem_v")
    q_m = proj(h, "mem_q", mw, BF16, "gnorm", group=MEM_DH, extras=[mq_gain], extra_specs=[gain_spec])
    o_mem = _mem_attention(q_m, m_k, m_v, tq=1024)

    gates = proj(h, "gates", N_BRANCH * d, BF16, "sigmoid")
    tm, tn = 512, 1024
    nj = d // tn
    merged = _matmul_wcast(
        [o_gla, o_sb, o_mem], [p["w_br_gla"], p["w_br_sb"], p["w_br_mem"]], [wcol(), wcol(), wcol()],
        [gates, gates, gates], [tile_spec(tm, tn, b * nj) for b in range(N_BRANCH)],
        n=d, tm=tm, tn=tn, out_dtype=BF16, epi="merge", name="branch_merge")
    x, h2 = _matmul_resid_norm(merged, p["w_o_bf16"], layer, x, p["ffn_norm"][layer], tm=512)

    tm, tn = 1024, 512
    act = _matmul_wcast([h2], [p["w_gate_up"], p["w_gate_up"]], [wcol(), wcol(d_ff // tn)], [], [],
                        n=d_ff, tm=tm, tn=tn, out_dtype=BF16, epi="swiglu", name="ffn_gate_up")
    tm, tn = 512, 1024
    x = _matmul_wcast([act], [p["w_down_bf16"]], [wcol()], [x], [tile_spec(tm, tn)],
                      n=d, tm=tm, tn=tn, out_dtype=F32, epi="resid", name="ffn_down", precast=True)
    return x


_PARAM_NAMES = ("attn_norm", "w_in", "gla_w_a2", "gla_b_a", "gla_out_norm", "w_br_gla", "sb_q_norm", "sb_k_norm",
                "w_br_sb", "mem_norm", "w_mem_kv", "mem_q_norm", "mem_k_norm", "w_br_mem", "w_o", "ffn_norm",
                "w_gate_up", "w_down")


def kernel(x, mem, attn_norm, w_in, gla_w_a2, gla_b_a, gla_out_norm, w_br_gla, sb_q_norm, sb_k_norm, w_br_sb,
           mem_norm, w_mem_kv, mem_q_norm, mem_k_norm, w_br_mem, w_o, ffn_norm, w_gate_up, w_down):
    params = dict(zip(_PARAM_NAMES, (attn_norm, w_in, gla_w_a2, gla_b_a, gla_out_norm, w_br_gla, sb_q_norm,
                                     sb_k_norm, w_br_sb, mem_norm, w_mem_kv, mem_q_norm, mem_k_norm, w_br_mem,
                                     w_o, ffn_norm, w_gate_up, w_down)))
    b, t, d = x.shape
    assert b == 1, "kernels are written for a single sequence"
    xs = x.reshape(t, d)
    ms = mem.reshape(mem.shape[1], d)
    params["w_in_t"] = jnp.swapaxes(w_in, 1, 2)
    params["w_down_bf16"] = _cast_bf16(w_down, tk=512)
    params["w_o_bf16"] = _cast_bf16(w_o, tk=512)
    for layer in range(w_in.shape[0]):
        xs = _layer(xs, ms, params, layer)
    return xs.reshape(b, t, d)
```

```python
import functools
import math

import jax
import jax.numpy as jnp
from jax import lax
from jax.experimental import pallas as pl
from jax.experimental.pallas import tpu as pltpu

F32 = jnp.float32
BF16 = jnp.bfloat16
EPS = 1e-6

GLA_HEADS = 4
GLA_DK = 128
GLA_DV = 256
GLA_RANK = 16
GLA_TAU = 16.0
GLA_CHUNK = 64
GLA_CUMSUM_ROWS = 256
SB_HEADS = 8
SB_DH = 128
MEM_HEADS = 4
MEM_DH = 256
N_BRANCH = 3

LANES = 128
VMEM_LIMIT_BYTES = 56 * 2**20
F32_EXP2_UNDERFLOW = 150.0
LOG2_E = math.log2(math.e)

_NT = (((1,), (1,)), ((), ()))
_TN = (((0,), (0,)), ((), ()))


def _dot(a, b):
    return jnp.dot(a, b, preferred_element_type=F32)


def _dot_nt(a, b):
    return lax.dot_general(a, b, _NT, preferred_element_type=F32)


def _softplus2(z2):
    return jnp.where(z2 > 64.0, z2, jnp.log2(1.0 + jnp.exp2(z2)))


def _sigmoid(z):
    return 0.5 * jnp.tanh(0.5 * z) + 0.5


def _split_bf16(v):
    hi = v.astype(BF16)
    lo = (v - hi.astype(F32)).astype(BF16)
    return hi, lo


def _cparams(*sem):
    return pltpu.CompilerParams(dimension_semantics=sem, vmem_limit_bytes=VMEM_LIMIT_BYTES)


def _rmsnorm_kernel(x_ref, g_ref, o_ref):
    x = x_ref[...]
    ms = jnp.mean(x * x, axis=-1, keepdims=True)
    o_ref[...] = (x * lax.rsqrt(ms + EPS) * g_ref[...]).astype(o_ref.dtype)


def _rmsnorm(x, g, *, tm):
    m, d = x.shape
    return pl.pallas_call(
        _rmsnorm_kernel,
        grid=(m // tm,),
        in_specs=[pl.BlockSpec((tm, d), lambda i: (i, 0)), pl.BlockSpec((1, d), lambda i: (0, 0))],
        out_specs=pl.BlockSpec((tm, d), lambda i: (i, 0)),
        out_shape=jax.ShapeDtypeStruct((m, d), BF16),
        compiler_params=_cparams("parallel"),
        name="rmsnorm",
    )(x, g.reshape(1, d))


def _cast_kernel(w_ref, o_ref):
    o_ref[...] = w_ref[...].astype(o_ref.dtype)


def _cast_bf16(w, *, tk):
    n_l, k, n = w.shape
    return pl.pallas_call(
        _cast_kernel,
        grid=(n_l, k // tk),
        in_specs=[pl.BlockSpec((None, tk, n), lambda l, i: (l, i, 0))],
        out_specs=pl.BlockSpec((None, tk, n), lambda l, i: (l, i, 0)),
        out_shape=jax.ShapeDtypeStruct(w.shape, BF16),
        compiler_params=_cparams("parallel", "parallel"),
        name="cast_bf16",
    )(w)


def _mm_body(a_refs, w_refs, extra, o_ref, epi, group, w_rows):
    mm = _dot_nt if w_rows else _dot
    if epi == "swiglu":
        a = a_refs[0][...]
        gate = mm(a, w_refs[0][...])
        up = mm(a, w_refs[1][...])
        o_ref[...] = (gate * _sigmoid(gate) * up).astype(o_ref.dtype)
        return
    if epi == "merge":
        out = None
        for a_ref, w_ref, g_ref in zip(a_refs, w_refs, extra):
            y = g_ref[...].astype(F32) * mm(a_ref[...], w_ref[...])
            out = y if out is None else out + y
        o_ref[...] = out.astype(o_ref.dtype)
        return
    acc = mm(a_refs[0][...], w_refs[0][...])
    if epi == "cast":
        o_ref[...] = acc.astype(o_ref.dtype)
    elif epi == "sigmoid":
        o_ref[...] = _sigmoid(acc).astype(o_ref.dtype)
    elif epi == "resid":
        o_ref[...] = (extra[0][...] + acc).astype(o_ref.dtype)
    elif epi == "gnorm":
        gain = extra[0][...]
        for c0 in range(0, acc.shape[1], group):
            y = acc[:, c0:c0 + group]
            ms = jnp.mean(y * y, axis=-1, keepdims=True)
            o_ref[:, c0:c0 + group] = (y * lax.rsqrt(ms + EPS) * gain[:, c0:c0 + group]).astype(o_ref.dtype)
    else:
        raise ValueError(epi)


def _mm_wcast_kernel(*refs, n_a, n_w, epi, group, w_rows, shift, precast):
    a_refs = refs[:n_a]
    w_refs = refs[n_a:n_a + n_w]
    if precast:
        _mm_body(a_refs, w_refs, refs[n_a + n_w:-1], refs[-1], epi, group, w_rows)
        return
    n_next = 1 if shift else 0
    next_refs = refs[n_a + n_w:n_a + n_w + n_next]
    extra = refs[n_a + n_w + n_next:-1 - n_w]
    o_ref = refs[-1 - n_w]
    wbf_refs = refs[-n_w:]

    @pl.when(pl.program_id(1) == 0)
    def _():
        if shift:
            tn = w_refs[0].shape[0]
            wbf_refs[0][0:tn - shift, :] = w_refs[0][shift:tn, :].astype(BF16)
            wbf_refs[0][tn - shift:tn, :] = next_refs[0][...].astype(BF16)
        else:
            for w_ref, wbf_ref in zip(w_refs, wbf_refs):
                wbf_ref[...] = w_ref[...].astype(BF16)

    _mm_body(a_refs, wbf_refs, extra, o_ref, epi, group, w_rows)


def _matmul_wcast(a_list, w_list, w_maps, extras, extra_specs, *, n, tm, tn, out_dtype, epi, name, group=0,
                  w_rows=False, shift=0, next_map=None, precast=False):
    m = a_list[0].shape[0]
    in_specs = [pl.BlockSpec((tm, a.shape[1]), lambda j, i: (i, 0)) for a in a_list]
    w_blocks = [(tn, w.shape[2]) if w_rows else (w.shape[1], tn) for w in w_list]
    in_specs += [pl.BlockSpec((None,) + blk, wm) for blk, wm in zip(w_blocks, w_maps)]
    operands = list(a_list) + list(w_list)
    if shift:
        assert w_rows and len(w_list) == 1 and shift % 16 == 0 and tn % shift == 0
        in_specs.append(pl.BlockSpec((None, shift, w_list[0].shape[2]), next_map))
        operands.append(w_list[0])
    in_specs += list(extra_specs)
    kern = functools.partial(_mm_wcast_kernel, n_a=len(a_list), n_w=len(w_list), epi=epi, group=group,
                             w_rows=w_rows, shift=shift, precast=precast)
    return pl.pallas_call(
        kern,
        grid=(n // tn, m // tm),
        in_specs=in_specs,
        out_specs=pl.BlockSpec((tm, tn), lambda j, i: (i, j)),
        out_shape=jax.ShapeDtypeStruct((m, n), out_dtype),
        scratch_shapes=[] if precast else [pltpu.VMEM(blk, BF16) for blk in w_blocks],
        compiler_params=_cparams("parallel", "arbitrary"),
        name=name,
    )(*operands, *extras)


def _resid_norm_kernel(a_ref, w_ref, x_ref, g_ref, xo_ref, ho_ref):
    x_new = x_ref[...] + _dot(a_ref[...], w_ref[...])
    xo_ref[...] = x_new
    ms = jnp.mean(x_new * x_new, axis=-1, keepdims=True)
    ho_ref[...] = (x_new * lax.rsqrt(ms + EPS) * g_ref[...]).astype(ho_ref.dtype)


def _matmul_resid_norm(a, w_bf16, layer, x, gain, *, tm):
    m, k = a.shape
    n = w_bf16.shape[2]
    return pl.pallas_call(
        _resid_norm_kernel,
        grid=(m // tm,),
        in_specs=[
            pl.BlockSpec((tm, k), lambda i: (i, 0)),
            pl.BlockSpec((None, k, n), lambda i: (layer, 0, 0)),
            pl.BlockSpec((tm, n), lambda i: (i, 0)),
            pl.BlockSpec((1, n), lambda i: (0, 0)),
        ],
        out_specs=[pl.BlockSpec((tm, n), lambda i: (i, 0)), pl.BlockSpec((tm, n), lambda i: (i, 0))],
        out_shape=[jax.ShapeDtypeStruct((m, n), F32), jax.ShapeDtypeStruct((m, n), BF16)],
        compiler_params=_cparams("parallel"),
        name="out_proj_norm",
    )(a, w_bf16, x, gain.reshape(1, n))


def _gla_kernel(q_ref, k_ref, v_ref, r_ref, h_ref, wa1_ref, wa2_ref, ba_ref, gn_ref, o_ref, s_ref, wa1_bf_ref, *, bt):
    c_len = GLA_CHUNK

    @pl.when(pl.program_id(0) == 0)
    def _():
        s_ref[...] = jnp.zeros_like(s_ref)
        wa1_bf_ref[...] = wa1_ref[...].astype(BF16)

    cum = min(bt, GLA_CUMSUM_ROWS)
    row = lax.broadcasted_iota(jnp.int32, (cum, cum), 0)
    col = lax.broadcasted_iota(jnp.int32, (cum, cum), 1)
    shift = c_len.bit_length() - 1
    same_chunk = (row >> shift) == (col >> shift)
    tri = jnp.logical_and(col <= row, same_chunk).astype(BF16)
    crow = lax.broadcasted_iota(jnp.int32, (c_len, c_len), 0)
    ccol = lax.broadcasted_iota(jnp.int32, (c_len, c_len), 1)
    causal = ccol <= crow

    a1 = _dot_nt(h_ref[...], wa1_bf_ref[...]).astype(BF16)
    pre = _dot(a1, wa2_ref[...]) + ba_ref[...]
    log_a = _softplus2(pre * -LOG2_E) * (-1.0 / GLA_TAU)
    la_hi, la_lo = _split_bf16(log_a)
    b = jnp.concatenate([_dot(tri, la_hi[r0:r0 + cum]) + _dot(tri, la_lo[r0:r0 + cum])
                         for r0 in range(0, bt, cum)], axis=0)

    gn = gn_ref[...]
    scale = GLA_DK ** -0.5
    n_chunks = bt // c_len
    units = [(c, h) for c in range(n_chunks) for h in range(GLA_HEADS)]
    prep = {}
    for c, h in units:
        rs = slice(c * c_len, (c + 1) * c_len)
        ks = slice(h * GLA_DK, (h + 1) * GLA_DK)
        vs = slice(h * GLA_DV, (h + 1) * GLA_DV)
        bh = b[rs, ks]
        b_last = bh[c_len - 1:c_len, :]
        k = k_ref[rs, ks]
        v = v_ref[rs, vs].astype(BF16)
        q_e = (q_ref[rs, ks] * (scale * jnp.exp2(bh))).astype(BF16)
        k_e = (k * jnp.exp2(-bh)).astype(BF16)
        k_d = (k * jnp.exp2(b_last - bh)).astype(BF16)
        prep[c, h] = (q_e, k_e, k_d, v, jnp.exp2(b_last))
    scores = {u: jnp.where(causal, _dot_nt(prep[u][0], prep[u][1]), 0.0).astype(BF16) for u in units}
    pre_s = {}
    for u in units:
        q_e, _, k_d, v, decay = prep[u]
        d_state = lax.dot_general(v, k_d, _TN, preferred_element_type=F32)
        pre_s[u] = (q_e, _dot(scores[u], v), d_state, decay)
    for h in range(GLA_HEADS):
        vs = slice(h * GLA_DV, (h + 1) * GLA_DV)
        state_t = s_ref[h]
        for c in range(n_chunks):
            rs = slice(c * c_len, (c + 1) * c_len)
            q_e, o_intra, d_state, decay = pre_s[c, h]
            o = o_intra + _dot_nt(q_e, state_t.astype(BF16))
            state_t = state_t * decay + d_state
            ms = jnp.mean(o * o, axis=-1, keepdims=True)
            o_n = o * lax.rsqrt(ms + EPS) * gn
            r = r_ref[rs, vs]
            o_ref[rs, vs] = (o_n * (r * _sigmoid(r))).astype(o_ref.dtype)
        s_ref[h] = state_t


def _gla(p_gla, h, w_in_t, a1_block, wa2p, ba, gn, *, bt):
    t, d = h.shape
    hk = GLA_HEADS * GLA_DK
    hv = GLA_HEADS * GLA_DV
    return pl.pallas_call(
        functools.partial(_gla_kernel, bt=bt),
        grid=(t // bt,),
        in_specs=[
            pl.BlockSpec((bt, hk), lambda i: (i, 0)),
            pl.BlockSpec((bt, hk), lambda i: (i, 1)),
            pl.BlockSpec((bt, hv), lambda i: (i, 1)),
            pl.BlockSpec((bt, hv), lambda i: (i, 2)),
            pl.BlockSpec((bt, d), lambda i: (i, 0)),
            pl.BlockSpec((None, LANES, d), lambda i: a1_block),
            pl.BlockSpec((LANES, hk), lambda i: (0, 0)),
            pl.BlockSpec((1, hk), lambda i: (0, 0)),
            pl.BlockSpec((1, GLA_DV), lambda i: (0, 0)),
        ],
        out_specs=pl.BlockSpec((bt, hv), lambda i: (i, 0)),
        out_shape=jax.ShapeDtypeStruct((t, hv), BF16),
        scratch_shapes=[pltpu.VMEM((GLA_HEADS, GLA_DV, GLA_DK), F32), pltpu.VMEM((LANES, d), BF16)],
        compiler_params=_cparams("arbitrary"),
        name="gla",
    )(p_gla, p_gla, p_gla, p_gla, h, w_in_t, wa2p, ba.reshape(1, hk), gn.reshape(1, GLA_DV))


def _sb_kernel(q_ref, k_ref, v_ref, o_ref, c_ref, acc_ref, *, tile, heads, n_sub):
    i = pl.program_id(1)
    row = lax.broadcasted_iota(jnp.int32, (tile, tile), 0)
    col = lax.broadcasted_iota(jnp.int32, (tile, tile), 1)
    from_here = (row >= col).astype(BF16)
    causal = col < row
    units = [(hd, sub) for hd in range(heads) for sub in range(n_sub)]

    def tile_step(hd, sub, j, c, acc, diag, guard):
        cs = slice(hd * SB_DH, (hd + 1) * SB_DH)
        start = pl.multiple_of((jnp.maximum(j, 0) if guard else j) * tile, tile)
        q = q_ref[sub * tile:(sub + 1) * tile, cs]
        k = k_ref[pl.ds(start, tile), cs]
        v = v_ref[pl.ds(start, tile), cs]
        z = _dot_nt(q, k)
        l = _softplus2(z)
        keep = causal if diag else (jnp.broadcast_to(j, (tile, tile)) >= 0 if guard else None)
        if keep is not None:
            l = jnp.where(keep, l, 0.0)
        e = z - _dot(l.astype(BF16), from_here)
        if c is not None:
            e = e - c
        a = jnp.exp2(e)
        if keep is not None:
            a = jnp.where(keep, a, 0.0)
        pv = _dot(a.astype(BF16), v)
        tot = jnp.sum(l, axis=-1, keepdims=True)
        if c is None:
            return tot, pv
        return c + tot, acc + pv

    def static_part(first_step):
        jobs = []
        for u, (hd, sub) in enumerate(units):
            g = i * n_sub + sub
            jobs.append((u, g, True))
            if not (first_step and sub == 0):
                jobs.append((u, g - 1, False))

        def operands(u, j):
            hd, sub = units[u]
            cs = slice(hd * SB_DH, (hd + 1) * SB_DH)
            start = pl.multiple_of(j * tile, tile)
            return q_ref[sub * tile:(sub + 1) * tile, cs], k_ref[pl.ds(start, tile), cs], v_ref[pl.ds(start, tile), cs]

        jobs.sort(key=lambda job: not job[2])
        zs = [_dot_nt(*operands(u, j)[:2]) for u, j, _ in jobs]
        ls = [jnp.where(causal, _softplus2(z), 0.0) if diag else _softplus2(z) for z, (_, _, diag) in zip(zs, jobs)]
        ws = [_dot(l.astype(BF16), from_here) for l in ls]
        tots = [jnp.sum(l, axis=-1, keepdims=True) for l in ls]
        state = [None] * len(units)
        for (u, j, diag), z, w, tot in zip(jobs, zs, ws, tots):
            v = operands(u, j)[2]
            if diag:
                a = jnp.where(causal, jnp.exp2(z - w), 0.0)
                state[u] = (tot, _dot(a.astype(BF16), v))
            else:
                c, acc = state[u]
                a = jnp.exp2(z - w - c)
                state[u] = (c + tot, acc + _dot(a.astype(BF16), v))
        return publish(state)

    def publish(state):
        for u, ((hd, sub), (c, acc)) in enumerate(zip(units, state)):
            c_ref[u] = c
            acc_ref[u] = acc
            o_ref[sub * tile:(sub + 1) * tile, hd * SB_DH:(hd + 1) * SB_DH] = acc.astype(o_ref.dtype)
        return functools.reduce(jnp.minimum, [jnp.min(c) for c, _ in state])

    c_min = lax.cond(i == 0, lambda: static_part(True), lambda: static_part(False))

    def cond(carry):
        n, c_min = carry
        j_newest = i * n_sub + (n_sub - 1) - 2 - n
        return jnp.logical_and(j_newest >= 0, c_min <= F32_EXP2_UNDERFLOW)

    def body(carry):
        n, _ = carry
        new = [tile_step(hd, sub, i * n_sub + sub - 2 - n, c_ref[u], acc_ref[u], False, True)
               for u, (hd, sub) in enumerate(units)]
        return n + 1, publish(new)

    lax.while_loop(cond, body, (jnp.int32(0), c_min))


def _sb_attention(qk, v, *, tile, heads, n_sub):
    t = v.shape[0]
    groups = SB_HEADS // heads
    w = heads * SB_DH
    rows = tile * n_sub
    return pl.pallas_call(
        functools.partial(_sb_kernel, tile=tile, heads=heads, n_sub=n_sub),
        grid=(groups, t // rows),
        in_specs=[
            pl.BlockSpec((rows, w), lambda g, i: (i, g)),
            pl.BlockSpec((t, w), lambda g, i: (0, groups + g)),
            pl.BlockSpec((t, w), lambda g, i: (0, g)),
        ],
        out_specs=pl.BlockSpec((rows, w), lambda g, i: (i, g)),
        out_shape=jax.ShapeDtypeStruct((t, SB_HEADS * SB_DH), BF16),
        scratch_shapes=[pltpu.VMEM((heads * n_sub, tile, 1), F32), pltpu.VMEM((heads * n_sub, tile, SB_DH), F32)],
        compiler_params=_cparams("parallel", "parallel"),
        name="sb_attention",
    )(qk, qk, v)


def _mem_kernel(q_ref, k_ref, v_ref, o_ref):
    cols = [slice(h * MEM_DH, (h + 1) * MEM_DH) for h in range(MEM_HEADS)]
    ss = [_dot_nt(q_ref[:, cs], k_ref[:, cs]) for cs in cols]
    es = [jnp.exp2(s - jnp.max(s, axis=-1, keepdims=True)) for s in ss]
    ps = [(e / jnp.sum(e, axis=-1, keepdims=True)).astype(BF16) for e in es]
    for cs, p in zip(cols, ps):
        o_ref[:, cs] = _dot(p, v_ref[:, cs]).astype(o_ref.dtype)


def _mem_attention(q, mk, mv, *, tq):
    t, w = q.shape
    m = mk.shape[0]
    return pl.pallas_call(
        _mem_kernel,
        grid=(t // tq,),
        in_specs=[
            pl.BlockSpec((tq, w), lambda i: (i, 0)),
            pl.BlockSpec((m, w), lambda i: (0, 0)),
            pl.BlockSpec((m, w), lambda i: (0, 0)),
        ],
        out_specs=pl.BlockSpec((tq, w), lambda i: (i, 0)),
        out_shape=jax.ShapeDtypeStruct((t, w), BF16),
        compiler_params=_cparams("parallel"),
        name="mem_attention",
    )(q, mk, mv)


PROJ_TN = 1024


def _w_in_segments(d):
    hk = GLA_HEADS * GLA_DK
    hv = GLA_HEADS * GLA_DV
    mix = SB_HEADS * SB_DH
    sizes = {"gla": 2 * hk + 2 * hv, "a1": GLA_RANK, "sb_qk": 2 * mix, "sb_v": mix, "mem_q": MEM_HEADS * MEM_DH,
             "gates": N_BRANCH * d}
    offs, src = {}, 0
    for name, size in sizes.items():
        offs[name] = src
        src += size
    return offs


def _layer(x, mem, p, layer):
    t, d = x.shape
    m = mem.shape[0]
    hk = GLA_HEADS * GLA_DK
    hv = GLA_HEADS * GLA_DV
    mix = SB_HEADS * SB_DH
    mw = MEM_HEADS * MEM_DH
    d_ff = p["w_down"].shape[1]
    seg = _w_in_segments(d)

    def wcol(off=0):
        return lambda j, i: (layer, 0, j + off)

    def proj(a, name, n, out_dtype, epi, *, tn=PROJ_TN, group=0, extras=(), extra_specs=()):
        shift = seg[name] % tn
        base = seg[name] - shift
        rows = lambda j, i: (layer, base // tn + j, 0)
        next_map = (lambda j, i: (layer, (base + (j + 1) * tn) // shift, 0)) if shift else None
        return _matmul_wcast([a], [p["w_in_t"]], [rows], list(extras), list(extra_specs), n=n, tm=1024, tn=tn,
                             out_dtype=out_dtype, epi=epi, group=group, name="proj_" + name, w_rows=True,
                             shift=shift, next_map=next_map)

    def tile_spec(tm, tn, off=0):
        return pl.BlockSpec((tm, tn), lambda j, i: (i, j + off))

    gain_spec = pl.BlockSpec((1, PROJ_TN), lambda j, i: (0, j))

    h = _rmsnorm(x, p["attn_norm"][layer], tm=512)

    p_gla = proj(h, "gla", 2 * hk + 2 * hv, F32, "cast")
    assert seg["a1"] % LANES == 0
    wa2p = jnp.pad(p["gla_w_a2"][layer], ((0, LANES - GLA_RANK), (0, 0))).astype(BF16)
    o_gla = _gla(p_gla, h, p["w_in_t"], (layer, seg["a1"] // LANES, 0), wa2p, p["gla_b_a"][layer],
                 p["gla_out_norm"][layer], bt=1024)

    q_gain = p["sb_q_norm"][layer] * (SB_DH ** -0.5 * LOG2_E)
    qk_gain = jnp.concatenate([jnp.tile(q_gain, SB_HEADS), jnp.tile(p["sb_k_norm"][layer], SB_HEADS)]).reshape(1, -1)
    qk = proj(h, "sb_qk", 2 * mix, BF16, "gnorm", group=SB_DH, extras=[qk_gain], extra_specs=[gain_spec])
    sv = proj(h, "sb_v", mix, BF16, "cast")
    o_sb = _sb_attention(qk, sv, tile=256, heads=2, n_sub=2)

    hm = _rmsnorm(mem, p["mem_norm"][layer], tm=m)
    mk_gain = jnp.tile(p["mem_k_norm"][layer], MEM_HEADS).reshape(1, -1)
    mq_gain = jnp.tile(p["mem_q_norm"][layer] * (MEM_DH ** -0.5 * LOG2_E), MEM_HEADS).reshape(1, -1)
    m_k = _matmul_wcast([hm], [p["w_mem_kv"]], [wcol()], [mk_gain], [gain_spec], n=mw, tm=m, tn=mw,
                        out_dtype=BF16, epi="gnorm", group=MEM_DH, name="proj_mem_k")
    m_v = _matmul_wcast([hm], [p["w_mem_kv"]], [wcol(1)], [], [], n=mw, tm=m, tn=mw, out_dtype=BF16, epi="cast",
                        name="proj_mem_v")
    q_m = proj(h, "mem_q", mw, BF16, "gnorm", group=MEM_DH, extras=[mq_gain], extra_specs=[gain_spec])
    o_mem = _mem_attention(q_m, m_k, m_v, tq=1024)

    gates = proj(h, "gates", N_BRANCH * d, BF16, "sigmoid")
    tm, tn = 512, 1024
    nj = d // tn
    merged = _matmul_wcast(
        [o_gla, o_sb, o_mem], [p["w_br_gla"], p["w_br_sb"], p["w_br_mem"]], [wcol(), wcol(), wcol()],
        [gates, gates, gates], [tile_spec(tm, tn, b * nj) for b in range(N_BRANCH)],
        n=d, tm=tm, tn=tn, out_dtype=BF16, epi="merge", name="branch_merge")
    x, h2 = _matmul_resid_norm(merged, p["w_o_bf16"], layer, x, p["ffn_norm"][layer], tm=512)

    tm, tn = 1024, 512
    act = _matmul_wcast([h2], [p["w_gate_up"], p["w_gate_up"]], [wcol(), wcol(d_ff // tn)], [], [],
                        n=d_ff, tm=tm, tn=tn, out_dtype=BF16, epi="swiglu", name="ffn_gate_up")
    tm, tn = 512, 1024
    x = _matmul_wcast([act], [p["w_down_bf16"]], [wcol()], [x], [tile_spec(tm, tn)],
                      n=d, tm=tm, tn=tn, out_dtype=F32, epi="resid", name="ffn_down", precast=True)
    return x


_PARAM_NAMES = ("attn_norm", "w_in", "gla_w_a2", "gla_b_a", "gla_out_norm", "w_br_gla", "sb_q_norm", "sb_k_norm",
                "w_br_sb", "mem_norm", "w_mem_kv", "mem_q_norm", "mem_k_norm", "w_br_mem", "w_o", "ffn_norm",
                "w_gate_up", "w_down")


def kernel(x, mem, attn_norm, w_in, gla_w_a2, gla_b_a, gla_out_norm, w_br_gla, sb_q_norm, sb_k_norm, w_br_sb,
           mem_norm, w_mem_kv, mem_q_norm, mem_k_norm, w_br_mem, w_o, ffn_norm, w_gate_up, w_down):
    params = dict(zip(_PARAM_NAMES, (attn_norm, w_in, gla_w_a2, gla_b_a, gla_out_norm, w_br_gla, sb_q_norm,
                                     sb_k_norm, w_br_sb, mem_norm, w_mem_kv, mem_q_norm, mem_k_norm, w_br_mem,
                                     w_o, ffn_norm, w_gate_up, w_down)))
    b, t, d = x.shape
    assert b == 1, "kernels are written for a single sequence"
    xs = x.reshape(t, d)
    ms = mem.reshape(mem.shape[1], d)
    params["w_in_t"] = jnp.swapaxes(w_in, 1, 2)
    params["w_down_bf16"] = _cast_bf16(w_down, tk=512)
    params["w_o_bf16"] = _cast_bf16(w_o, tk=512)
    for layer in range(w_in.shape[0]):
        xs = _layer(xs, ms, params, layer)
    return xs.reshape(b, t, d)
```

```python
import functools
import math

import jax
import jax.numpy as jnp
from jax import lax
from jax.experimental import pallas as pl
from jax.experimental.pallas import tpu as pltpu

F32 = jnp.float32
BF16 = jnp.bfloat16
EPS = 1e-6

GLA_HEADS = 4
GLA_DK = 128
GLA_DV = 256
GLA_RANK = 16
GLA_TAU = 16.0
GLA_CHUNK = 64
GLA_CUMSUM_ROWS = 256
SB_HEADS = 8
SB_DH = 128
MEM_HEADS = 4
MEM_DH = 256
N_BRANCH = 3

LANES = 128
VMEM_LIMIT_BYTES = 56 * 2**20
F32_EXP2_UNDERFLOW = 150.0
LOG2_E = math.log2(math.e)

_NT = (((1,), (1,)), ((), ()))
_TN = (((0,), (0,)), ((), ()))


def _dot(a, b):
    return jnp.dot(a, b, preferred_element_type=F32)


def _dot_nt(a, b):
    return lax.dot_general(a, b, _NT, preferred_element_type=F32)


def _softplus2(z2):
    return jnp.where(z2 > 64.0, z2, jnp.log2(1.0 + jnp.exp2(z2)))


def _sigmoid(z):
    return 0.5 * jnp.tanh(0.5 * z) + 0.5


def _split_bf16(v):
    hi = v.astype(BF16)
    lo = (v - hi.astype(F32)).astype(BF16)
    return hi, lo


def _cparams(*sem):
    return pltpu.CompilerParams(dimension_semantics=sem, vmem_limit_bytes=VMEM_LIMIT_BYTES)


def _rmsnorm_kernel(x_ref, g_ref, o_ref):
    x = x_ref[...]
    ms = jnp.mean(x * x, axis=-1, keepdims=True)
    o_ref[...] = (x * lax.rsqrt(ms + EPS) * g_ref[...]).astype(o_ref.dtype)


def _rmsnorm(x, g, *, tm):
    m, d = x.shape
    return pl.pallas_call(
        _rmsnorm_kernel,
        grid=(m // tm,),
        in_specs=[pl.BlockSpec((tm, d), lambda i: (i, 0)), pl.BlockSpec((1, d), lambda i: (0, 0))],
        out_specs=pl.BlockSpec((tm, d), lambda i: (i, 0)),
        out_shape=jax.ShapeDtypeStruct((m, d), BF16),
        compiler_params=_cparams("parallel"),
        name="rmsnorm",
    )(x, g.reshape(1, d))


def _cast_kernel(w_ref, o_ref):
    o_ref[...] = w_ref[...].astype(o_ref.dtype)


def _cast_bf16(w, *, tk):
    n_l, k, n = w.shape
    return pl.pallas_call(
        _cast_kernel,
        grid=(n_l, k // tk),
        in_specs=[pl.BlockSpec((None, tk, n), lambda l, i: (l, i, 0))],
        out_specs=pl.BlockSpec((None, tk, n), lambda l, i: (l, i, 0)),
        out_shape=jax.ShapeDtypeStruct(w.shape, BF16),
        compiler_params=_cparams("parallel", "parallel"),
        name="cast_bf16",
    )(w)


def _mm_body(a_refs, w_refs, extra, o_ref, epi, group, w_rows):
    mm = _dot_nt if w_rows else _dot
    if epi == "swiglu":
        a = a_refs[0][...]
        gate = mm(a, w_refs[0][...])
        up = mm(a, w_refs[1][...])
        o_ref[...] = (gate * _sigmoid(gate) * up).astype(o_ref.dtype)
        return
    if epi == "merge":
        out = None
        for a_ref, w_ref, g_ref in zip(a_refs, w_refs, extra):
            y = g_ref[...].astype(F32) * mm(a_ref[...], w_ref[...])
            out = y if out is None else out + y
        o_ref[...] = out.astype(o_ref.dtype)
        return
    acc = mm(a_refs[0][...], w_refs[0][...])
    if epi == "cast":
        o_ref[...] = acc.astype(o_ref.dtype)
    elif epi == "sigmoid":
        o_ref[...] = _sigmoid(acc).astype(o_ref.dtype)
    elif epi == "resid":
        o_ref[...] = (extra[0][...] + acc).astype(o_ref.dtype)
    elif epi == "gnorm":
        gain = extra[0][...]
        for c0 in range(0, acc.shape[1], group):
            y = acc[:, c0:c0 + group]
            ms = jnp.mean(y * y, axis=-1, keepdims=True)
            o_ref[:, c0:c0 + group] = (y * lax.rsqrt(ms + EPS) * gain[:, c0:c0 + group]).astype(o_ref.dtype)
    else:
        raise ValueError(epi)


def _mm_wcast_kernel(*refs, n_a, n_w, epi, group, w_rows, shift, precast):
    a_refs = refs[:n_a]
    w_refs = refs[n_a:n_a + n_w]
    if precast:
        _mm_body(a_refs, w_refs, refs[n_a + n_w:-1], refs[-1], epi, group, w_rows)
        return
    n_next = 1 if shift else 0
    next_refs = refs[n_a + n_w:n_a + n_w + n_next]
    extra = refs[n_a + n_w + n_next:-1 - n_w]
    o_ref = refs[-1 - n_w]
    wbf_refs = refs[-n_w:]

    @pl.when(pl.program_id(1) == 0)
    def _():
        if shift:
            tn = w_refs[0].shape[0]
            wbf_refs[0][0:tn - shift, :] = w_refs[0][shift:tn, :].astype(BF16)
            wbf_refs[0][tn - shift:tn, :] = next_refs[0][...].astype(BF16)
        else:
            for w_ref, wbf_ref in zip(w_refs, wbf_refs):
                wbf_ref[...] = w_ref[...].astype(BF16)

    _mm_body(a_refs, wbf_refs, extra, o_ref, epi, group, w_rows)


def _matmul_wcast(a_list, w_list, w_maps, extras, extra_specs, *, n, tm, tn, out_dtype, epi, name, group=0,
                  w_rows=False, shift=0, next_map=None, precast=False):
    m = a_list[0].shape[0]
    in_specs = [pl.BlockSpec((tm, a.shape[1]), lambda j, i: (i, 0)) for a in a_list]
    w_blocks = [(tn, w.shape[2]) if w_rows else (w.shape[1], tn) for w in w_list]
    in_specs += [pl.BlockSpec((None,) + blk, wm) for blk, wm in zip(w_blocks, w_maps)]
    operands = list(a_list) + list(w_list)
    if shift:
        assert w_rows and len(w_list) == 1 and shift % 16 == 0 and tn % shift == 0
        in_specs.append(pl.BlockSpec((None, shift, w_list[0].shape[2]), next_map))
        operands.append(w_list[0])
    in_specs += list(extra_specs)
    kern = functools.partial(_mm_wcast_kernel, n_a=len(a_list), n_w=len(w_list), epi=epi, group=group,
                             w_rows=w_rows, shift=shift, precast=precast)
    return pl.pallas_call(
        kern,
        grid=(n // tn, m // tm),
        in_specs=in_specs,
        out_specs=pl.BlockSpec((tm, tn), lambda j, i: (i, j)),
        out_shape=jax.ShapeDtypeStruct((m, n), out_dtype),
        scratch_shapes=[] if precast else [pltpu.VMEM(blk, BF16) for blk in w_blocks],
        compiler_params=_cparams("parallel", "arbitrary"),
        name=name,
    )(*operands, *extras)


def _resid_norm_kernel(a_ref, w_ref, x_ref, g_ref, xo_ref, ho_ref):
    x_new = x_ref[...] + _dot(a_ref[...], w_ref[...])
    xo_ref[...] = x_new
    ms = jnp.mean(x_new * x_new, axis=-1, keepdims=True)
    ho_ref[...] = (x_new * lax.rsqrt(ms + EPS) * g_ref[...]).astype(ho_ref.dtype)


def _matmul_resid_norm(a, w_bf16, layer, x, gain, *, tm):
    m, k = a.shape
    n = w_bf16.shape[2]
    return pl.pallas_call(
        _resid_norm_kernel,
        grid=(m // tm,),
        in_specs=[
            pl.BlockSpec((tm, k), lambda i: (i, 0)),
            pl.BlockSpec((None, k, n), lambda i: (layer, 0, 0)),
            pl.BlockSpec((tm, n), lambda i: (i, 0)),
            pl.BlockSpec((1, n), lambda i: (0, 0)),
        ],
        out_specs=[pl.BlockSpec((tm, n), lambda i: (i, 0)), pl.BlockSpec((tm, n), lambda i: (i, 0))],
        out_shape=[jax.ShapeDtypeStruct((m, n), F32), jax.ShapeDtypeStruct((m, n), BF16)],
        compiler_params=_cparams("parallel"),
        name="out_proj_norm",
    )(a, w_bf16, x, gain.reshape(1, n))


def _gla_kernel(q_ref, k_ref, v_ref, r_ref, h_ref, wa1_ref, wa2_ref, ba_ref, gn_ref, o_ref, s_ref, wa1_bf_ref, *, bt):
    c_len = GLA_CHUNK

    @pl.when(pl.program_id(0) == 0)
    def _():
        s_ref[...] = jnp.zeros_like(s_ref)
        wa1_bf_ref[...] = wa1_ref[...].astype(BF16)

    cum = min(bt, GLA_CUMSUM_ROWS)
    row = lax.broadcasted_iota(jnp.int32, (cum, cum), 0)
    col = lax.broadcasted_iota(jnp.int32, (cum, cum), 1)
    shift = c_len.bit_length() - 1
    same_chunk = (row >> shift) == (col >> shift)
    tri = jnp.logical_and(col <= row, same_chunk).astype(BF16)
    crow = lax.broadcasted_iota(jnp.int32, (c_len, c_len), 0)
    ccol = lax.broadcasted_iota(jnp.int32, (c_len, c_len), 1)
    causal = ccol <= crow

    a1 = _dot_nt(h_ref[...], wa1_bf_ref[...]).astype(BF16)
    pre = _dot(a1, wa2_ref[...]) + ba_ref[...]
    log_a = _softplus2(pre * -LOG2_E) * (-1.0 / GLA_TAU)
    la_hi, la_lo = _split_bf16(log_a)
    b = jnp.concatenate([_dot(tri, la_hi[r0:r0 + cum]) + _dot(tri, la_lo[r0:r0 + cum])
                         for r0 in range(0, bt, cum)], axis=0)

    gn = gn_ref[...]
    scale = GLA_DK ** -0.5
    n_chunks = bt // c_len
    units = [(c, h) for c in range(n_chunks) for h in range(GLA_HEADS)]
    prep = {}
    for c, h in units:
        rs = slice(c * c_len, (c + 1) * c_len)
        ks = slice(h * GLA_DK, (h + 1) * GLA_DK)
        vs = slice(h * GLA_DV, (h + 1) * GLA_DV)
        bh = b[rs, ks]
        b_last = bh[c_len - 1:c_len, :]
        k = k_ref[rs, ks]
        v = v_ref[rs, vs].astype(BF16)
        q_e = (q_ref[rs, ks] * (scale * jnp.exp2(bh))).astype(BF16)
        k_e = (k * jnp.exp2(-bh)).astype(BF16)
        k_d = (k * jnp.exp2(b_last - bh)).astype(BF16)
        prep[c, h] = (q_e, k_e, k_d, v, jnp.exp2(b_last))
    scores = {u: jnp.where(causal, _dot_nt(prep[u][0], prep[u][1]), 0.0).astype(BF16) for u in units}
    pre_s = {}
    for u in units:
        q_e, _, k_d, v, decay = prep[u]
        d_state = lax.dot_general(v, k_d, _TN, preferred_element_type=F32)
        pre_s[u] = (q_e, _dot(scores[u], v), d_state, decay)
    for h in range(GLA_HEADS):
        vs = slice(h * GLA_DV, (h + 1) * GLA_DV)
        state_t = s_ref[h]
        for c in range(n_chunks):
            rs = slice(c * c_len, (c + 1) * c_len)
            q_e, o_intra, d_state, decay = pre_s[c, h]
            o = o_intra + _dot_nt(q_e, state_t.astype(BF16))
            state_t = state_t * decay + d_state
            ms = jnp.mean(o * o, axis=-1, keepdims=True)
            o_n = o * lax.rsqrt(ms + EPS) * gn
            r = r_ref[rs, vs]
            o_ref[rs, vs] = (o_n * (r * _sigmoid(r))).astype(o_ref.dtype)
        s_ref[h] = state_t


def _gla(p_gla, h, w_in_t, a1_block, wa2p, ba, gn, *, bt):
    t, d = h.shape
    hk = GLA_HEADS * GLA_DK
    hv = GLA_HEADS * GLA_DV
    return pl.pallas_call(
        functools.partial(_gla_kernel, bt=bt),
        grid=(t // bt,),
        in_specs=[
            pl.BlockSpec((bt, hk), lambda i: (i, 0)),
            pl.BlockSpec((bt, hk), lambda i: (i, 1)),
            pl.BlockSpec((bt, hv), lambda i: (i, 1)),
            pl.BlockSpec((bt, hv), lambda i: (i, 2)),
            pl.BlockSpec((bt, d), lambda i: (i, 0)),
            pl.BlockSpec((None, LANES, d), lambda i: a1_block),
            pl.BlockSpec((LANES, hk), lambda i: (0, 0)),
            pl.BlockSpec((1, hk), lambda i: (0, 0)),
            pl.BlockSpec((1, GLA_DV), lambda i: (0, 0)),
        ],
        out_specs=pl.BlockSpec((bt, hv), lambda i: (i, 0)),
        out_shape=jax.ShapeDtypeStruct((t, hv), BF16),
        scratch_shapes=[pltpu.VMEM((GLA_HEADS, GLA_DV, GLA_DK), F32), pltpu.VMEM((LANES, d), BF16)],
        compiler_params=_cparams("arbitrary"),
        name="gla",
    )(p_gla, p_gla, p_gla, p_gla, h, w_in_t, wa2p, ba.reshape(1, hk), gn.reshape(1, GLA_DV))


def _sb_kernel(q_ref, k_ref, v_ref, o_ref, c_ref, acc_ref, *, tile, heads, n_sub):
    i = pl.program_id(1)
    row = lax.broadcasted_iota(jnp.int32, (tile, tile), 0)
    col = lax.broadcasted_iota(jnp.int32, (tile, tile), 1)
    from_here = (row >= col).astype(BF16)
    causal = col < row
    units = [(hd, sub) for hd in range(heads) for sub in range(n_sub)]

    def tile_step(hd, sub, j, c, acc, diag, guard):
        cs = slice(hd * SB_DH, (hd + 1) * SB_DH)
        start = pl.multiple_of((jnp.maximum(j, 0) if guard else j) * tile, tile)
        q = q_ref[sub * tile:(sub + 1) * tile, cs]
        k = k_ref[pl.ds(start, tile), cs]
        v = v_ref[pl.ds(start, tile), cs]
        z = _dot_nt(q, k)
        l = _softplus2(z)
        keep = causal if diag else (jnp.broadcast_to(j, (tile, tile)) >= 0 if guard else None)
        if keep is not None:
            l = jnp.where(keep, l, 0.0)
        e = z - _dot(l.astype(BF16), from_here)
        if c is not None:
            e = e - c
        a = jnp.exp2(e)
        if keep is not None:
            a = jnp.where(keep, a, 0.0)
        pv = _dot(a.astype(BF16), v)
        tot = jnp.sum(l, axis=-1, keepdims=True)
        if c is None:
            return tot, pv
        return c + tot, acc + pv

    def static_part(first_step):
        jobs = []
        for u, (hd, sub) in enumerate(units):
            g = i * n_sub + sub
            jobs.append((u, g, True))
            if not (first_step and sub == 0):
                jobs.append((u, g - 1, False))

        def operands(u, j):
            hd, sub = units[u]
            cs = slice(hd * SB_DH, (hd + 1) * SB_DH)
            start = pl.multiple_of(j * tile, tile)
            return q_ref[sub * tile:(sub + 1) * tile, cs], k_ref[pl.ds(start, tile), cs], v_ref[pl.ds(start, tile), cs]

        state = [None] * len(units)
        for hd_now in range(heads):
            group = sorted((job for job in jobs if units[job[0]][0] == hd_now), key=lambda job: not job[2])
            zs = [_dot_nt(*operands(u, j)[:2]) for u, j, _ in group]
            ls = [jnp.where(causal, _softplus2(z), 0.0) if diag else _softplus2(z)
                  for z, (_, _, diag) in zip(zs, group)]
            ws = [_dot(l.astype(BF16), from_here) for l in ls]
            tots = [jnp.sum(l, axis=-1, keepdims=True) for l in ls]
            for (u, j, diag), z, w, tot in zip(group, zs, ws, tots):
                v = operands(u, j)[2]
                if diag:
                    a = jnp.where(causal, jnp.exp2(z - w), 0.0)
                    state[u] = (tot, _dot(a.astype(BF16), v))
                else:
                    c, acc = state[u]
                    a = jnp.exp2(z - w - c)
                    state[u] = (c + tot, acc + _dot(a.astype(BF16), v))
        return publish(state)

    def publish(state):
        for u, ((hd, sub), (c, acc)) in enumerate(zip(units, state)):
            c_ref[u] = c
            acc_ref[u] = acc
            o_ref[sub * tile:(sub + 1) * tile, hd * SB_DH:(hd + 1) * SB_DH] = acc.astype(o_ref.dtype)
        return functools.reduce(jnp.minimum, [jnp.min(c) for c, _ in state])

    c_min = lax.cond(i == 0, lambda: static_part(True), lambda: static_part(False))

    def cond(carry):
        n, c_min = carry
        j_newest = i * n_sub + (n_sub - 1) - 2 - n
        return jnp.logical_and(j_newest >= 0, c_min <= F32_EXP2_UNDERFLOW)

    def body(carry):
        n, _ = carry
        new = [tile_step(hd, sub, i * n_sub + sub - 2 - n, c_ref[u], acc_ref[u], False, True)
               for u, (hd, sub) in enumerate(units)]
        return n + 1, publish(new)

    lax.while_loop(cond, body, (jnp.int32(0), c_min))


def _sb_attention(qk, v, *, tile, heads, n_sub):
    t = v.shape[0]
    groups = SB_HEADS // heads
    w = heads * SB_DH
    rows = tile * n_sub
    return pl.pallas_call(
        functools.partial(_sb_kernel, tile=tile, heads=heads, n_sub=n_sub),
        grid=(groups, t // rows),
        in_specs=[
            pl.BlockSpec((rows, w), lambda g, i: (i, g)),
            pl.BlockSpec((t, w), lambda g, i: (0, groups + g)),
            pl.BlockSpec((t, w), lambda g, i: (0, g)),
        ],
        out_specs=pl.BlockSpec((rows, w), lambda g, i: (i, g)),
        out_shape=jax.ShapeDtypeStruct((t, SB_HEADS * SB_DH), BF16),
        scratch_shapes=[pltpu.VMEM((heads * n_sub, tile, 1), F32), pltpu.VMEM((heads * n_sub, tile, SB_DH), F32)],
        compiler_params=_cparams("parallel", "parallel"),
        name="sb_attention",
    )(qk, qk, v)


def _mem_kernel(q_ref, k_ref, v_ref, o_ref):
    cols = [slice(h * MEM_DH, (h + 1) * MEM_DH) for h in range(MEM_HEADS)]
    ss = [_dot_nt(q_ref[:, cs], k_ref[:, cs]) for cs in cols]
    es = [jnp.exp2(s - jnp.max(s, axis=-1, keepdims=True)) for s in ss]
    ps = [(e / jnp.sum(e, axis=-1, keepdims=True)).astype(BF16) for e in es]
    for cs, p in zip(cols, ps):
        o_ref[:, cs] = _dot(p, v_ref[:, cs]).astype(o_ref.dtype)


def _mem_attention(q, mk, mv, *, tq):
    t, w = q.shape
    m = mk.shape[0]
    return pl.pallas_call(
        _mem_kernel,
        grid=(t // tq,),
        in_specs=[
            pl.BlockSpec((tq, w), lambda i: (i, 0)),
            pl.BlockSpec((m, w), lambda i: (0, 0)),
            pl.BlockSpec((m, w), lambda i: (0, 0)),
        ],
        out_specs=pl.BlockSpec((tq, w), lambda i: (i, 0)),
        out_shape=jax.ShapeDtypeStruct((t, w), BF16),
        compiler_params=_cparams("parallel"),
        name="mem_attention",
    )(q, mk, mv)


PROJ_TN = 1024


def _w_in_segments(d):
    hk = GLA_HEADS * GLA_DK
    hv = GLA_HEADS * GLA_DV
    mix = SB_HEADS * SB_DH
    sizes = {"gla": 2 * hk + 2 * hv, "a1": GLA_RANK, "sb_qk": 2 * mix, "sb_v": mix, "mem_q": MEM_HEADS * MEM_DH,
             "gates": N_BRANCH * d}
    offs, src = {}, 0
    for name, size in sizes.items():
        offs[name] = src
        src += size
    return offs


def _layer(x, mem, p, layer):
    t, d = x.shape
    m = mem.shape[0]
    hk = GLA_HEADS * GLA_DK
    hv = GLA_HEADS * GLA_DV
    mix = SB_HEADS * SB_DH
    mw = MEM_HEADS * MEM_DH
    d_ff = p["w_down"].shape[1]
    seg = _w_in_segments(d)

    def wcol(off=0):
        return lambda j, i: (layer, 0, j + off)

    def proj(a, name, n, out_dtype, epi, *, tn=PROJ_TN, group=0, extras=(), extra_specs=()):
        shift = seg[name] % tn
        base = seg[name] - shift
        rows = lambda j, i: (layer, base // tn + j, 0)
        next_map = (lambda j, i: (layer, (base + (j + 1) * tn) // shift, 0)) if shift else None
        return _matmul_wcast([a], [p["w_in_t"]], [rows], list(extras), list(extra_specs), n=n, tm=1024, tn=tn,
                             out_dtype=out_dtype, epi=epi, group=group, name="proj_" + name, w_rows=True,
                             shift=shift, next_map=next_map)

    def tile_spec(tm, tn, off=0):
        return pl.BlockSpec((tm, tn), lambda j, i: (i, j + off))

    gain_spec = pl.BlockSpec((1, PROJ_TN), lambda j, i: (0, j))

    h = _rmsnorm(x, p["attn_norm"][layer], tm=512)

    p_gla = proj(h, "gla", 2 * hk + 2 * hv, F32, "cast")
    assert seg["a1"] % LANES == 0
    wa2p = jnp.pad(p["gla_w_a2"][layer], ((0, LANES - GLA_RANK), (0, 0))).astype(BF16)
    o_gla = _gla(p_gla, h, p["w_in_t"], (layer, seg["a1"] // LANES, 0), wa2p, p["gla_b_a"][layer],
                 p["gla_out_norm"][layer], bt=1024)

    q_gain = p["sb_q_norm"][layer] * (SB_DH ** -0.5 * LOG2_E)
    qk_gain = jnp.concatenate([jnp.tile(q_gain, SB_HEADS), jnp.tile(p["sb_k_norm"][layer], SB_HEADS)]).reshape(1, -1)
    qk = proj(h, "sb_qk", 2 * mix, BF16, "gnorm", group=SB_DH, extras=[qk_gain], extra_specs=[gain_spec])
    sv = proj(h, "sb_v", mix, BF16, "cast")
    o_sb = _sb_attention(qk, sv, tile=256, heads=2, n_sub=2)

    hm = _rmsnorm(mem, p["mem_norm"][layer], tm=m)
    mk_gain = jnp.tile(p["mem_k_norm"][layer], MEM_HEADS).reshape(1, -1)
    mq_gain = jnp.tile(p["mem_q_norm"][layer] * (MEM_DH ** -0.5 * LOG2_E), MEM_HEADS).reshape(1, -1)
    m_k = _matmul_wcast([hm], [p["w_mem_kv"]], [wcol()], [mk_gain], [gain_spec], n=mw, tm=m, tn=mw,
                        out_dtype=BF16, epi="gnorm", group=MEM_DH, name="proj_mem_k")
    m_v = _matmul_wcast([hm], [p["w_mem_kv"]], [wcol(1)], [], [], n=mw, tm=m, tn=mw, out_dtype=BF16, epi="cast",
                        name="proj_mem_v")
    q_m = proj(h, "mem_q", mw, BF16, "gnorm", group=MEM_DH, extras=[mq_gain], extra_specs=[gain_spec])
    o_mem = _mem_attention(q_m, m_k, m_v, tq=1024)

    gates = proj(h, "gates", N_BRANCH * d, BF16, "sigmoid")
    tm, tn = 512, 1024
    nj = d // tn
    merged = _matmul_wcast(
        [o_gla, o_sb, o_mem], [p["w_br_gla"], p["w_br_sb"], p["w_br_mem"]], [wcol(), wcol(), wcol()],
        [gates, gates, gates], [tile_spec(tm, tn, b * nj) for b in range(N_BRANCH)],
        n=d, tm=tm, tn=tn, out_dtype=BF16, epi="merge", name="branch_merge")
    x, h2 = _matmul_resid_norm(merged, p["w_o_bf16"], layer, x, p["ffn_norm"][layer], tm=512)

    tm, tn = 1024, 512
    act = _matmul_wcast([h2], [p["w_gate_up"], p["w_gate_up"]], [wcol(), wcol(d_ff // tn)], [], [],
                        n=d_ff, tm=tm, tn=tn, out_dtype=BF16, epi="swiglu", name="ffn_gate_up")
    tm, tn = 512, 1024
    x = _matmul_wcast([act], [p["w_down_bf16"]], [wcol()], [x], [tile_spec(tm, tn)],
                      n=d, tm=tm, tn=tn, out_dtype=F32, epi="resid", name="ffn_down", precast=True)
    return x


_PARAM_NAMES = ("attn_norm", "w_in", "gla_w_a2", "gla_b_a", "gla_out_norm", "w_br_gla", "sb_q_norm", "sb_k_norm",
                "w_br_sb", "mem_norm", "w_mem_kv", "mem_q_norm", "mem_k_norm", "w_br_mem", "w_o", "ffn_norm",
                "w_gate_up", "w_down")


def kernel(x, mem, attn_norm, w_in, gla_w_a2, gla_b_a, gla_out_norm, w_br_gla, sb_q_norm, sb_k_norm, w_br_sb,
           mem_norm, w_mem_kv, mem_q_norm, mem_k_norm, w_br_mem, w_o, ffn_norm, w_gate_up, w_down):
    params = dict(zip(_PARAM_NAMES, (attn_norm, w_in, gla_w_a2, gla_b_a, gla_out_norm, w_br_gla, sb_q_norm,
                                     sb_k_norm, w_br_sb, mem_norm, w_mem_kv, mem_q_norm, mem_k_norm, w_br_mem,
                                     w_o, ffn_norm, w_gate_up, w_down)))
    b, t, d = x.shape
    assert b == 1, "kernels are written for a single sequence"
    xs = x.reshape(t, d)
    ms = mem.reshape(mem.shape[1], d)
    params["w_in_t"] = jnp.swapaxes(w_in, 1, 2)
    params["w_down_bf16"] = _cast_bf16(w_down, tk=512)
    params["w_o_bf16"] = _cast_bf16(w_o, tk=512)
    for layer in range(w_in.shape[0]):
        xs = _layer(xs, ms, params, layer)
    return xs.reshape(b, t, d)
```
